```python
import math
import jax, jax.numpy as jnp
from jax import lax
import numpy as np

D_MODEL = 2048
BATCH = 8
SEQ = 4096
DEPTH = 4

D_MIX = D_MODEL
HEAD_DIM = 128
HGRN_WIDTH = D_MIX // 4
GDN_WIDTH = D_MIX // 2
POOL_WIDTH = D_MIX - HGRN_WIDTH - GDN_WIDTH
HGRN_HEADS = HGRN_WIDTH // HEAD_DIM
GDN_HEADS = GDN_WIDTH // HEAD_DIM
POOL_WINDOWS = (2, 4, 8, 16)
POOL_GROUP = POOL_WIDTH // len(POOL_WINDOWS)
POOL_MAX = max(POOL_WINDOWS)
CONV_WIDTH = 4
CHUNK = 64
D_FF = 256 * ((8 * D_MODEL // 3 + 255) // 256)
NORM_EPS = 1e-6
L2_EPS = 1e-6
IN_SPLITS = (HGRN_WIDTH,) * 4 + (GDN_WIDTH,) * 4 + (GDN_HEADS, GDN_HEADS, POOL_WIDTH)
D_IN = sum(IN_SPLITS)
IN_OFFSETS = tuple(int(o) for o in np.cumsum(IN_SPLITS)[:-1])

kernel_name = "hymba_style_hgrn2_gdn_pool_macaron"


def rms_norm(x, w):
    xf = x.astype(jnp.float32)
    y = xf * lax.rsqrt(jnp.mean(xf * xf, axis=-1, keepdims=True) + NORM_EPS)
    return (y * w.astype(jnp.float32)).astype(x.dtype)


def l2_normalize(x):
    return x * lax.rsqrt(jnp.sum(x * x, axis=-1, keepdims=True) + L2_EPS)


def swiglu(h, w_gate, w_up, w_down):
    return (jax.nn.silu(h @ w_gate) * (h @ w_up)) @ w_down


def to_chunks(a):
    b, t, h = a.shape[:3]
    a = a.reshape(b, t // CHUNK, CHUNK, h, *a.shape[3:])
    return jnp.moveaxis(a, (1, 3), (0, 2))


def from_chunks(a):
    n, b, h, c, d = a.shape
    return jnp.moveaxis(a, (0, 2), (1, 3)).reshape(b, n * c, h, d)


def causal_masks():
    idx = jnp.arange(CHUNK)
    return idx[:, None] >= idx[None, :], idx[:, None] > idx[None, :]


def hgrn2_chunked(q, k, v, log_f):
    b, t, h, dk = q.shape
    dv = v.shape[-1]
    causal, _ = causal_masks()

    def step(S, inp):
        qc, kc, vc, gc = inp
        cum = jnp.cumsum(gc, axis=-2)
        diff = cum[:, :, :, None, :] - cum[:, :, None, :, :]
        decay = jnp.exp(jnp.where(causal[:, :, None], diff, -jnp.inf))
        scores = jnp.einsum('bhtk,bhsk,bhtsk->bhts', qc, kc, decay)
        o = (jnp.einsum('bhts,bhsv->bhtv', scores, vc)
             + jnp.einsum('bhtk,bhkv->bhtv', qc * jnp.exp(cum), S))
        last = cum[:, :, -1:, :]
        S = (S * jnp.exp(last)[:, :, 0, :, None]
             + jnp.einsum('bhsk,bhsv->bhkv', kc * jnp.exp(last - cum), vc))
        return S, o

    S0 = jnp.zeros((b, h, dk, dv), jnp.float32)
    _, o = lax.scan(step, S0, (to_chunks(q), to_chunks(k), to_chunks(v), to_chunks(log_f)))
    return from_chunks(o)


def hgrn2_mixer(q, f, i, g, lb, norm_w):
    bsz, t, _ = q.shape
    heads = lambda a: a.reshape(bsz, t, HGRN_HEADS, HEAD_DIM)
    q = jax.nn.silu(q.astype(jnp.float32))
    z = f.astype(jnp.float32)
    log_f = jnp.logaddexp(jnp.log(lb), jnp.log1p(-lb) + jax.nn.log_sigmoid(z))
    k = (1.0 - lb) * jax.nn.sigmoid(-z)
    o = hgrn2_chunked(heads(q), heads(k), heads(i.astype(jnp.float32)), heads(log_f))
    o = rms_norm(o, norm_w) * jax.nn.silu(heads(g.astype(jnp.float32)))
    return o.reshape(bsz, t, HGRN_WIDTH)


def causal_depthwise_conv(u, w):
    c = u.shape[-1]
    return lax.conv_general_dilated(u, w[:, None, :], window_strides=(1,),
                                    padding=((CONV_WIDTH - 1, 0),),
                                    dimension_numbers=('NWC', 'WIO', 'NWC'),
                                    feature_group_count=c)


def gated_delta_chunked(q, k, v, g, beta):
    b, t, h, dk = q.shape
    dv = v.shape[-1]
    q, k, v = to_chunks(q), to_chunks(k), to_chunks(v)
    g, beta = to_chunks(g), to_chunks(beta)
    causal, strict = causal_masks()
    G = jnp.cumsum(g, axis=-1)
    decay = jnp.exp(jnp.where(causal, G[..., :, None] - G[..., None, :], -jnp.inf))
    kb = k * beta[..., None]
    L = jnp.where(strict, jnp.einsum('nbhik,nbhjk->nbhij', kb, k) * decay, 0.0)
    A = L + jnp.eye(CHUNK, dtype=L.dtype)
    rhs = jnp.concatenate([v * beta[..., None], kb * jnp.exp(G)[..., None]], axis=-1)
    sol = lax.linalg.triangular_solve(A, rhs, left_side=True, lower=True, unit_diagonal=True)
    u, w = sol[..., :dv], sol[..., dv:]
    attn = jnp.einsum('nbhik,nbhjk->nbhij', q, k) * decay
    q_dec = q * jnp.exp(G)[..., None]
    k_dec = k * jnp.exp(G[..., -1:] - G)[..., None]
    last = jnp.exp(G[..., -1])

    def step(S, inp):
        u_n, w_n, qd_n, kd_n, attn_n, last_n = inp
        v_new = u_n - jnp.einsum('bhck,bhkv->bhcv', w_n, S)
        o = (jnp.einsum('bhck,bhkv->bhcv', qd_n, S)
             + jnp.einsum('bhij,bhjv->bhiv', attn_n, v_new))
        S = S * last_n[..., None, None] + jnp.einsum('bhck,bhcv->bhkv', kd_n, v_new)
        return S, o

    S0 = jnp.zeros((b, h, dk, dv), jnp.float32)
    _, o = lax.scan(step, S0, (u, w, q_dec, k_dec, attn, last))
    return from_chunks(o)


def gdn_mixer(q, k, v, gate, b_logit, a_logit, conv_w, a_log, dt_bias, norm_w):
    bsz, t, _ = q.shape
    heads = lambda a: a.reshape(bsz, t, GDN_HEADS, HEAD_DIM)
    qkv = jnp.concatenate([q, k, v], axis=-1).astype(jnp.float32)
    qkv = jax.nn.silu(causal_depthwise_conv(qkv, conv_w.astype(jnp.float32)))
    q, k, v = jnp.split(qkv, 3, axis=-1)
    q = l2_normalize(heads(q)) * (HEAD_DIM ** -0.5)
    k = l2_normalize(heads(k))
    v = heads(v)
    beta = jax.nn.sigmoid(b_logit.astype(jnp.float32))
    g = -jnp.exp(a_log.astype(jnp.float32)) * jax.nn.softplus(
        a_logit.astype(jnp.float32) + dt_bias.astype(jnp.float32))
    o = gated_delta_chunked(q, k, v, g, beta)
    o = rms_norm(o, norm_w) * jax.nn.silu(heads(gate.astype(jnp.float32)))
    return o.reshape(bsz, t, GDN_WIDTH)


def pool_mixer(u, pool_w, pool_scale):
    uf = u.astype(jnp.float32)
    t = uf.shape[1]
    csum = jnp.pad(jnp.cumsum(uf, axis=1), ((0, 0), (POOL_MAX, 0), (0, 0)))
    pos = jnp.arange(t)
    outs = []
    for gi, win in enumerate(POOL_WINDOWS):
        sl = slice(gi * POOL_GROUP, (gi + 1) * POOL_GROUP)
        cg = csum[:, :, sl]
        window_sum = cg[:, POOL_MAX:] - cg[:, POOL_MAX - win:POOL_MAX - win + t]
        count = jnp.minimum(pos + 1, win).astype(jnp.float32)[None, :, None]
        m = window_sum / count - uf[:, :, sl]
        outs.append(jnp.einsum('btc,cd->btd', m, pool_w[gi].astype(jnp.float32)))
    return jnp.concatenate(outs, axis=-1) * pool_scale.astype(jnp.float32)


def hybrid_mixer(h, lb, w_in, conv_w, a_log, dt_bias, hgrn_norm_w, gdn_norm_w, pool_w, pool_scale, w_out):
    p = jnp.einsum('btd,de->bte', h, w_in)
    hq, hf, hi, hg, gq, gk, gv, gg, gb, ga, pu = jnp.split(p, IN_OFFSETS, axis=-1)
    y_a = hgrn2_mixer(hq, hf, hi, hg, lb, hgrn_norm_w)
    y_b = gdn_mixer(gq, gk, gv, gg, gb, ga, conv_w, a_log, dt_bias, gdn_norm_w)
    y_c = pool_mixer(pu, pool_w, pool_scale)
    y = jnp.concatenate([y_a, y_b, y_c], axis=-1).astype(h.dtype)
    return jnp.einsum('bte,ed->btd', y, w_out)


def setup_inputs(seed: int = 0) -> dict:
    key = jax.random.key(seed)
    ks = jax.random.split(key, 24)
    nrm = lambda k, shape, scale: jax.random.normal(k, shape, jnp.float32) * scale
    gain = lambda k, shape: 1.0 + 0.05 * jax.random.normal(k, shape, jnp.float32)
    dt = jnp.exp(jax.random.uniform(ks[10], (DEPTH, GDN_HEADS), jnp.float32,
                                    minval=math.log(1e-3), maxval=math.log(1e-1)))
    return {
        "x": nrm(ks[0], (BATCH, SEQ, D_MODEL), 1.0),
        "lb_logits": nrm(ks[1], (DEPTH, HGRN_WIDTH), 0.5),
        "norm_ffn1": gain(ks[2], (DEPTH, D_MODEL)),
        "ffn1_w_gate": nrm(ks[3], (DEPTH, D_MODEL, D_FF), D_MODEL ** -0.5),
        "ffn1_w_up": nrm(ks[4], (DEPTH, D_MODEL, D_FF), D_MODEL ** -0.5),
        "ffn1_w_down": nrm(ks[5], (DEPTH, D_FF, D_MODEL), D_FF ** -0.5),
        "norm_mix": gain(ks[6], (DEPTH, D_MODEL)),
        "w_in": nrm(ks[7], (DEPTH, D_MODEL, D_IN), D_MODEL ** -0.5),
        "gdn_conv_w": nrm(ks[8], (DEPTH, CONV_WIDTH, 3 * GDN_WIDTH), CONV_WIDTH ** -0.5),
        "gdn_a_log": jnp.log(jax.random.uniform(ks[9], (DEPTH, GDN_HEADS), jnp.float32, minval=1.0, maxval=16.0)),
        "gdn_dt_bias": dt + jnp.log(-jnp.expm1(-dt)),
        "hgrn_norm_w": gain(ks[11], (DEPTH, HEAD_DIM)),
        "gdn_norm_w": gain(ks[12], (DEPTH, HEAD_DIM)),
        "pool_w": nrm(ks[13], (DEPTH, len(POOL_WINDOWS), POOL_GROUP, POOL_GROUP), POOL_GROUP ** -0.5),
        "pool_scale": 1.0 + 0.1 * jax.random.normal(ks[14], (DEPTH, POOL_WIDTH), jnp.float32),
        "w_out": nrm(ks[15], (DEPTH, D_MIX, D_MODEL), D_MIX ** -0.5),
        "norm_ffn2": gain(ks[16], (DEPTH, D_MODEL)),
        "ffn2_w_gate": nrm(ks[17], (DEPTH, D_MODEL, D_FF), D_MODEL ** -0.5),
        "ffn2_w_up": nrm(ks[18], (DEPTH, D_MODEL, D_FF), D_MODEL ** -0.5),
        "ffn2_w_down": nrm(ks[19], (DEPTH, D_FF, D_MODEL), D_FF ** -0.5),
        "norm_final": gain(ks[20], (D_MODEL,)),
    }


def reference(x, lb_logits, norm_ffn1, ffn1_w_gate, ffn1_w_up, ffn1_w_down, norm_mix, w_in,
              gdn_conv_w, gdn_a_log, gdn_dt_bias, hgrn_norm_w, gdn_norm_w, pool_w, pool_scale,
              w_out, norm_ffn2, ffn2_w_gate, ffn2_w_up, ffn2_w_down, norm_final):
    lbs = jnp.cumsum(jax.nn.softmax(lb_logits.astype(jnp.float32), axis=0), axis=0)
    lbs = lbs - lbs[0:1]
    for l in range(DEPTH):
        x = x + 0.5 * swiglu(rms_norm(x, norm_ffn1[l]), ffn1_w_gate[l], ffn1_w_up[l], ffn1_w_down[l])
        x = x + hybrid_mixer(rms_norm(x, norm_mix[l]), lbs[l], w_in[l], gdn_conv_w[l], gdn_a_log[l],
                             gdn_dt_bias[l], hgrn_norm_w[l], gdn_norm_w[l], pool_w[l], pool_scale[l], w_out[l])
        x = x + 0.5 * swiglu(rms_norm(x, norm_ffn2[l]), ffn2_w_gate[l], ffn2_w_up[l], ffn2_w_down[l])
    return rms_norm(x, norm_final)
```

```python
import functools

import numpy as np
import jax
import jax.numpy as jnp
from jax import lax
from jax.experimental import pallas as pl
from jax.experimental.pallas import tpu as pltpu

HEAD_DIM = 128
POOL_WINDOWS = (2, 4, 8, 16)
POOL_MAX = max(POOL_WINDOWS)
CONV_WIDTH = 4
NORM_EPS = 1e-6
L2_EPS = 1e-6

LANES = 128
SUBLANES = 8
VMEM_LIMIT_BYTES = 56 * 1024 * 1024

CHUNK = 128
N_LEVELS = 7

F32 = jnp.float32
BF16 = jnp.bfloat16


def _params(sem):
    return pltpu.CompilerParams(dimension_semantics=sem, vmem_limit_bytes=VMEM_LIMIT_BYTES)


def _rms(x, w):
    return x * lax.rsqrt(jnp.mean(x * x, axis=-1, keepdims=True) + NORM_EPS) * w


def _silu(x):
    return x * jax.nn.sigmoid(x)


def _dot(a, b):
    return jnp.dot(a.astype(BF16), b.astype(BF16), preferred_element_type=F32)


def _dot_nt(a, b):
    return lax.dot_general(a.astype(BF16), b.astype(BF16), (((1,), (1,)), ((), ())),
                           preferred_element_type=F32)


def _dot_tn(a, b):
    return lax.dot_general(a.astype(BF16), b.astype(BF16), (((0,), (0,)), ((), ())),
                           preferred_element_type=F32)


def _dot_f32(a, b):
    return jnp.dot(a, b, preferred_element_type=F32, precision=lax.Precision.HIGHEST)


def _split3(x):
    hi = x.astype(BF16)
    r = x - hi.astype(F32)
    mid = r.astype(BF16)
    lo = (r - mid.astype(F32)).astype(BF16)
    return jnp.concatenate([hi, mid, lo], axis=0)


def _ffn_kernel(x_ref, nw_ref, wg_ref, wu_ref, wd_ref, fw_ref, o_ref, h_ref, acc_ref, *, final_norm):
    f = pl.program_id(1)

    @pl.when(f == 0)
    def _():
        h_ref[...] = _rms(x_ref[...], nw_ref[...]).astype(BF16)

    h = h_ref[...]
    g = jnp.dot(h, wg_ref[...], preferred_element_type=F32)
    u = jnp.dot(h, wu_ref[...], preferred_element_type=F32)
    part = jnp.dot((_silu(g) * u).astype(BF16), wd_ref[...], preferred_element_type=F32)

    @pl.when(f == 0)
    def _():
        acc_ref[...] = part

    @pl.when(f > 0)
    def _():
        acc_ref[...] += part

    @pl.when(f == pl.num_programs(1) - 1)
    def _():
        y = x_ref[...] + 0.5 * acc_ref[...]
        if final_norm:
            y = _rms(y, fw_ref[...])
        o_ref[...] = y


def _ffn(x, nw, wg, wu, wd, fw, *, final_norm, tm, tf):
    n, d = x.shape
    dff = wg.shape[1]
    tm = min(tm, n)
    return pl.pallas_call(
        functools.partial(_ffn_kernel, final_norm=final_norm),
        out_shape=jax.ShapeDtypeStruct((n, d), F32),
        grid=(n // tm, dff // tf),
        in_specs=[
            pl.BlockSpec((tm, d), lambda i, f: (i, 0)),
            pl.BlockSpec((1, d), lambda i, f: (0, 0)),
            pl.BlockSpec((d, tf), lambda i, f: (0, f)),
            pl.BlockSpec((d, tf), lambda i, f: (0, f)),
            pl.BlockSpec((tf, d), lambda i, f: (f, 0)),
            pl.BlockSpec((1, d), lambda i, f: (0, 0)),
        ],
        out_specs=pl.BlockSpec((tm, d), lambda i, f: (i, 0)),
        scratch_shapes=[pltpu.VMEM((tm, d), BF16), pltpu.VMEM((tm, d), F32)],
        compiler_params=_params(("parallel", "arbitrary")),
        name="ffn",
    )(x, nw, wg, wu, wd, fw)


def _inproj_kernel(x_ref, nw_ref, w_ref, o_ref, h_ref):
    @pl.when(pl.program_id(1) == 0)
    def _():
        h_ref[...] = _rms(x_ref[...], nw_ref[...]).astype(BF16)

    o_ref[...] = jnp.dot(h_ref[...], w_ref[...], preferred_element_type=F32)


def _inproj(x, nw, w, *, tm, tn):
    n, d = x.shape
    e = w.shape[1]
    tm = min(tm, n)
    return pl.pallas_call(
        _inproj_kernel,
        out_shape=jax.ShapeDtypeStruct((n, e), F32),
        grid=(n // tm, e // tn),
        in_specs=[
            pl.BlockSpec((tm, d), lambda i, j: (i, 0)),
            pl.BlockSpec((1, d), lambda i, j: (0, 0)),
            pl.BlockSpec((d, tn), lambda i, j: (0, j)),
        ],
        out_specs=pl.BlockSpec((tm, tn), lambda i, j: (i, j)),
        scratch_shapes=[pltpu.VMEM((tm, d), BF16)],
        compiler_params=_params(("parallel", "arbitrary")),
        name="inproj",
    )(x, nw, w)


def _outproj_kernel(x_ref, ya_ref, yb_ref, yc_ref, wa_ref, wb_ref, wc_ref, o_ref):
    acc = jnp.dot(ya_ref[...], wa_ref[...], preferred_element_type=F32)
    acc += jnp.dot(yb_ref[...], wb_ref[...], preferred_element_type=F32)
    acc += jnp.dot(yc_ref[...], wc_ref[...], preferred_element_type=F32)
    o_ref[...] = x_ref[...] + acc


def _outproj(x, ya, yb, yc, wa, wb, wc, *, tm):
    n, d = x.shape
    tm = min(tm, n)
    row = lambda a: pl.BlockSpec((tm, a.shape[1]), lambda i: (i, 0))
    full = lambda a: pl.BlockSpec(a.shape, lambda i: (0, 0))
    return pl.pallas_call(
        _outproj_kernel,
        out_shape=jax.ShapeDtypeStruct((n, d), F32),
        grid=(n // tm,),
        in_specs=[row(x), row(ya), row(yb), row(yc), full(wa), full(wb), full(wc)],
        out_specs=row(x),
        compiler_params=_params(("parallel",)),
        name="outproj",
    )(x, ya, yb, yc, wa, wb, wc)


def _lb_kernel(x_ref, o_ref):
    depth = x_ref.shape[0]
    x = x_ref[...]
    e = jnp.exp(x - jnp.max(x, axis=0, keepdims=True))
    sm = e / jnp.sum(e, axis=0, keepdims=True)
    run = sm[0:1]
    first = run
    for l in range(depth):
        if l > 0:
            run = run + sm[l:l + 1]
        lb = run - first
        o_ref[l] = jnp.concatenate([jnp.log(lb), jnp.log1p(-lb), 1.0 - lb], axis=0)


def _lb_params(lb_logits):
    depth, w = lb_logits.shape
    return pl.pallas_call(
        _lb_kernel,
        out_shape=jax.ShapeDtypeStruct((depth, 3, w), F32),
        name="hgrn_lower_bounds",
    )(lb_logits)


def _hgrn_decay_matrix():
    c = CHUNK
    t = np.arange(c)[:, None]
    u = np.arange(c)[None, :]
    blocks = [(u <= t), (u > t)]
    for k in range(N_LEVELS - 1, -1, -1):
        s = 1 << k
        ref = (t // (2 * s)) * (2 * s) + s - 1
        odd = (t & s) != 0
        blocks.append(np.where(odd, (u > ref) & (u <= t), (u > t) & (u <= ref)))
    w = np.concatenate(blocks, axis=0).astype(np.float32)
    return np.concatenate([w, w, w], axis=1)


def _hgrn_kernel(q_ref, f_ref, i_ref, g_ref, lb_ref, nw_ref, w_ref, o_ref, st_ref):
    c = CHUNK
    tb = q_ref.shape[0]

    @pl.when(pl.program_id(2) == 0)
    def _():
        st_ref[...] = jnp.zeros_like(st_ref)

    log_lb = lb_ref[0:1, :]
    log_1m = lb_ref[1:2, :]
    one_m = lb_ref[2:3, :]
    nw = nw_ref[...]

    row = lax.broadcasted_iota(jnp.int32, (c, 1), 0)
    tt = lax.broadcasted_iota(jnp.int32, (c, c), 0)
    uu = lax.broadcasted_iota(jnp.int32, (c, c), 1)
    x = tt ^ uu
    hb = jnp.zeros((c, c), jnp.int32)
    for k in range(1, N_LEVELS):
        hb = hb + (x >= (1 << k)).astype(jnp.int32)
    hb = jnp.where(tt > uu, hb, -1)

    def chunk(ci, st):
        r0 = pl.multiple_of(ci * c, c)
        rows = pl.ds(r0, c)
        q = q_ref[rows, :]
        z = f_ref[rows, :]
        v = i_ref[rows, :]
        gate = g_ref[rows, :]

        qh = _silu(q)
        y = log_1m + (jnp.minimum(z, 0.0) - jnp.log1p(jnp.exp(-jnp.abs(z))))
        log_f = jnp.maximum(log_lb, y) + jnp.log1p(jnp.exp(-jnp.abs(log_lb - y)))
        kk = one_m * jax.nn.sigmoid(-z)

        dall = jnp.dot(w_ref[...], _split3(log_f), preferred_element_type=F32)
        cum = dall[0:c]
        e_cum = jnp.exp(cum)
        e_last = jnp.exp(dall[c:2 * c])

        p = jnp.zeros((c, c), F32)
        for li in range(N_LEVELS):
            k = N_LEVELS - 1 - li
            e = jnp.exp(dall[(2 + li) * c:(3 + li) * c])
            odd = (row & (1 << k)) != 0
            xs = jnp.where(odd, qh, kk) * e
            ps = _dot_nt(jnp.where(odd, xs, 0.0), jnp.where(odd, 0.0, xs))
            p = p + jnp.where(hb == k, ps, 0.0)

        diag = jnp.sum(qh * kk, axis=-1, keepdims=True)
        o = _dot_nt(qh * e_cum, st) + _dot(p, v) + diag * v
        st = st * jnp.exp(cum[c - 1:c, :]) + _dot_tn(v, kk * e_last)

        o_ref[rows, :] = (_rms(o, nw) * _silu(gate)).astype(o_ref.dtype)
        return st

    st_ref[...] = lax.fori_loop(0, tb // c, chunk, st_ref[...])


def _hgrn(p, lbp, nw, wmat, *, heads, tb):
    b, t, _ = p.shape
    tb = min(tb, t)
    col = lambda off: pl.BlockSpec((None, tb, LANES), lambda bi, h, ti: (bi, ti, off + h))
    return pl.pallas_call(
        _hgrn_kernel,
        out_shape=jax.ShapeDtypeStruct((b, t, heads * HEAD_DIM), BF16),
        grid=(b, heads, t // tb),
        in_specs=[
            col(0), col(heads), col(2 * heads), col(3 * heads),
            pl.BlockSpec((3, LANES), lambda bi, h, ti: (0, h)),
            pl.BlockSpec((1, LANES), lambda bi, h, ti: (0, 0)),
            pl.BlockSpec(wmat.shape, lambda bi, h, ti: (0, 0)),
        ],
        out_specs=pl.BlockSpec((None, tb, LANES), lambda bi, h, ti: (bi, ti, h)),
        scratch_shapes=[pltpu.VMEM((HEAD_DIM, HEAD_DIM), F32)],
        compiler_params=_params(("parallel", "parallel", "arbitrary")),
        name="hgrn2",
    )(p, p, p, p, lbp, nw, wmat)


def _gdn_kernel(q_ref, k_ref, v_ref, g_ref, gm_ref, cq_ref, ck_ref, cv_ref, ap_ref, nw_ref, tri_ref,
                o_ref, s_ref, halo_ref, ext_ref, qn_ref, kn_ref, vn_ref, *, head_off):
    c = CHUNK
    tb = q_ref.shape[0]
    h = pl.program_id(1)
    pad = SUBLANES

    @pl.when(pl.program_id(2) == 0)
    def _():
        s_ref[...] = jnp.zeros_like(s_ref)
        halo_ref[...] = jnp.zeros_like(halo_ref)

    def conv(idx, src_ref, cw_ref):
        ext_ref[0:pad, :] = halo_ref[idx]
        ext_ref[pad:pad + tb, :] = src_ref[...]
        halo_ref[idx] = src_ref[tb - pad:tb, :]
        acc = cw_ref[CONV_WIDTH - 1:CONV_WIDTH, :] * ext_ref[pad:pad + tb, :]
        for j in range(CONV_WIDTH - 1):
            sh = CONV_WIDTH - 1 - j
            acc = acc + cw_ref[j:j + 1, :] * ext_ref[pad - sh:pad - sh + tb, :]
        return _silu(acc)

    def l2n(a):
        return a * lax.rsqrt(jnp.sum(a * a, axis=-1, keepdims=True) + L2_EPS)

    qn_ref[...] = l2n(conv(0, q_ref, cq_ref)) * (HEAD_DIM ** -0.5)
    kn_ref[...] = l2n(conv(1, k_ref, ck_ref))
    vn_ref[...] = conv(2, v_ref, cv_ref)

    neg_a = -jnp.exp(ap_ref[0:1, :])
    dt_b = ap_ref[1:2, :]
    nw = nw_ref[...]
    lane = lax.broadcasted_iota(jnp.int32, (1, LANES), 1)
    oh_beta = (lane == h).astype(F32)
    oh_g = (lane == h + head_off).astype(F32)
    tt = lax.broadcasted_iota(jnp.int32, (c, c), 0)
    uu = lax.broadcasted_iota(jnp.int32, (c, c), 1)
    eye = (tt == uu).astype(F32)

    def chunk(ci, s):
        r0 = pl.multiple_of(ci * c, c)
        rows = pl.ds(r0, c)
        q = qn_ref[rows, :]
        k = kn_ref[rows, :]
        v = vn_ref[rows, :]
        gate = g_ref[rows, :]
        gm = gm_ref[rows, :]

        xg = gm + dt_b
        g_all = neg_a * (jnp.maximum(xg, 0.0) + jnp.log1p(jnp.exp(-jnp.abs(xg))))
        gc_all = jnp.dot(tri_ref[...], _split3(g_all), preferred_element_type=F32)
        beta = jnp.sum(jax.nn.sigmoid(gm) * oh_beta, axis=-1, keepdims=True)
        gcum = jnp.sum(gc_all * oh_g, axis=-1, keepdims=True)
        g_last = gcum[c - 1:c, :]

        gb = jnp.broadcast_to(gcum, (c, c))
        decay = jnp.exp(jnp.where(tt >= uu, gb - gb.T, -jnp.inf))
        kb = k * beta
        m = _dot_nt(jnp.concatenate([kb, q], axis=0), k)
        nmat = jnp.where(tt > uu, -(m[0:c] * decay), 0.0)
        attn = m[c:2 * c] * decay

        tm = eye + nmat
        pw = nmat
        for _ in range(N_LEVELS - 1):
            pw = _dot_f32(pw, pw)
            tm = tm + _dot_f32(tm, pw)

        e_g = jnp.exp(gcum)
        sol = _dot_f32(tm, jnp.concatenate([v * beta, kb * e_g], axis=1))
        u = sol[:, 0:HEAD_DIM]
        w = sol[:, HEAD_DIM:2 * HEAD_DIM]

        ws = _dot(jnp.concatenate([w, q * e_g], axis=0), s)
        v_new = u - ws[0:c]
        o = ws[c:2 * c] + _dot(attn, v_new)
        s = s * jnp.exp(g_last) + _dot_tn(k * jnp.exp(g_last - gcum), v_new)

        o_ref[rows, :] = (_rms(o, nw) * _silu(gate)).astype(o_ref.dtype)
        return s

    s_ref[...] = lax.fori_loop(0, tb // c, chunk, s_ref[...])


def _gdn(p, conv_w, ap, nw, tri, *, heads, col0, gm_col, tb):
    b, t, _ = p.shape
    tb = min(tb, t)
    col = lambda off: pl.BlockSpec((None, tb, LANES), lambda bi, h, ti: (bi, ti, off + h))
    cw = lambda off: pl.BlockSpec((CONV_WIDTH, LANES), lambda bi, h, ti: (0, off + h))
    const = lambda a: pl.BlockSpec(a.shape, lambda bi, h, ti: (0,) * a.ndim)
    return pl.pallas_call(
        functools.partial(_gdn_kernel, head_off=heads),
        out_shape=jax.ShapeDtypeStruct((b, t, heads * HEAD_DIM), BF16),
        grid=(b, heads, t // tb),
        in_specs=[
            col(col0), col(col0 + heads), col(col0 + 2 * heads), col(col0 + 3 * heads),
            pl.BlockSpec((None, tb, LANES), lambda bi, h, ti: (bi, ti, gm_col)),
            cw(0), cw(heads), cw(2 * heads),
            const(ap), const(nw), const(tri),
        ],
        out_specs=pl.BlockSpec((None, tb, LANES), lambda bi, h, ti: (bi, ti, h)),
        scratch_shapes=[
            pltpu.VMEM((HEAD_DIM, HEAD_DIM), F32),
            pltpu.VMEM((3, SUBLANES, LANES), F32),
            pltpu.VMEM((tb + SUBLANES, LANES), F32),
            pltpu.VMEM((tb, LANES), F32),
            pltpu.VMEM((tb, LANES), F32),
            pltpu.VMEM((tb, LANES), F32),
        ],
        compiler_params=_params(("parallel", "parallel", "arbitrary")),
        name="gated_deltanet",
    )(p, p, p, p, p, conv_w, conv_w, conv_w, ap, nw, tri)


def _pool_kernel(u_ref, w_ref, sc_ref, o_ref, ext_ref):
    tb = u_ref.shape[0]
    ti = pl.program_id(1)

    @pl.when(ti == 0)
    def _():
        ext_ref[0:POOL_MAX, :] = jnp.zeros((POOL_MAX, ext_ref.shape[1]), F32)

    ext_ref[POOL_MAX:POOL_MAX + tb, :] = u_ref[...]
    pos = ti * tb + lax.broadcasted_iota(jnp.int32, (tb, 1), 0)
    for gi, win in enumerate(POOL_WINDOWS):
        sl = slice(gi * LANES, (gi + 1) * LANES)
        cur = ext_ref[POOL_MAX:POOL_MAX + tb, sl]
        ws = cur
        for j in range(1, win):
            ws = ws + ext_ref[POOL_MAX - j:POOL_MAX - j + tb, sl]
        count = jnp.minimum(pos + 1, win).astype(F32)
        m = ws / count - cur
        o_ref[:, sl] = (_dot(m, w_ref[gi]) * sc_ref[:, sl]).astype(o_ref.dtype)
    ext_ref[0:POOL_MAX, :] = ext_ref[tb:tb + POOL_MAX, :]


def _pool(p, pool_w, scale, *, colblk, tb):
    b, t, _ = p.shape
    tb = min(tb, t)
    width = len(POOL_WINDOWS) * LANES
    return pl.pallas_call(
        _pool_kernel,
        out_shape=jax.ShapeDtypeStruct((b, t, width), BF16),
        grid=(b, t // tb),
        in_specs=[
            pl.BlockSpec((None, tb, width), lambda bi, ti: (bi, ti, colblk)),
            pl.BlockSpec(pool_w.shape, lambda bi, ti: (0, 0, 0)),
            pl.BlockSpec((1, width), lambda bi, ti: (0, 0)),
        ],
        out_specs=pl.BlockSpec((None, tb, width), lambda bi, ti: (bi, ti, 0)),
        scratch_shapes=[pltpu.VMEM((tb + POOL_MAX, width), F32)],
        compiler_params=_params(("parallel", "arbitrary")),
        name="causal_pool",
    )(p, pool_w, scale)


def kernel(x, lb_logits, norm_ffn1, ffn1_w_gate, ffn1_w_up, ffn1_w_down, norm_mix, w_in, gdn_conv_w, gdn_a_log, gdn_dt_bias, hgrn_norm_w, gdn_norm_w, pool_w, pool_scale, w_out, norm_ffn2, ffn2_w_gate, ffn2_w_up, ffn2_w_down, norm_final):
    bsz, seq, d = x.shape
    depth = lb_logits.shape[0]
    hgrn_w = lb_logits.shape[1]
    gdn_heads = gdn_a_log.shape[1]
    gdn_w = gdn_heads * HEAD_DIM
    pool_width = pool_scale.shape[1]
    hgrn_heads = hgrn_w // HEAD_DIM
    n = bsz * seq

    main = 4 * hgrn_w + 4 * gdn_w
    e_used = main + pool_width + LANES
    tn_in = 1152
    e_pad = -(-e_used // tn_in) * tn_in
    scal = w_in[:, :, main:main + 2 * gdn_heads]
    w_in_r = jnp.concatenate(
        [w_in[:, :, :main], w_in[:, :, main + 2 * gdn_heads:], scal,
         jnp.zeros((depth, d, e_pad - e_used + LANES - 2 * gdn_heads), w_in.dtype)], axis=-1).astype(BF16)
    hgrn_col0 = 0
    gdn_col0 = 4 * hgrn_heads
    pool_colblk = main // pool_width
    gm_col = (main + pool_width) // LANES

    lane_pad = lambda a: jnp.pad(a, ((0, 0), (gdn_heads, LANES - 2 * gdn_heads)))
    gdn_ap = jnp.stack([lane_pad(gdn_a_log), lane_pad(gdn_dt_bias)], axis=1)

    hgrn_wmat = jnp.asarray(_hgrn_decay_matrix(), BF16)
    tri = np.tril(np.ones((CHUNK, CHUNK), np.float32))
    tri3 = jnp.asarray(np.concatenate([tri, tri, tri], axis=1), BF16)

    lbp = _lb_params(lb_logits.astype(F32))

    wa = w_out[:, :hgrn_w].astype(BF16)
    wb = w_out[:, hgrn_w:hgrn_w + gdn_w].astype(BF16)
    wc = w_out[:, hgrn_w + gdn_w:].astype(BF16)

    xf = x.reshape(n, d)
    row = lambda a: a.reshape(1, -1)
    for l in range(depth):
        xf = _ffn(xf, row(norm_ffn1[l]), ffn1_w_gate[l].astype(BF16), ffn1_w_up[l].astype(BF16),
                  ffn1_w_down[l].astype(BF16), row(norm_final), final_norm=False, tm=512, tf=512)
        p = _inproj(xf, row(norm_mix[l]), w_in_r[l], tm=1024, tn=tn_in).reshape(bsz, seq, e_pad)
        ya = _hgrn(p, lbp[l], row(hgrn_norm_w[l]), hgrn_wmat, heads=hgrn_heads, tb=512)
        yb = _gdn(p, gdn_conv_w[l], gdn_ap[l], row(gdn_norm_w[l]), tri3,
                  heads=gdn_heads, col0=gdn_col0, gm_col=gm_col, tb=512)
        yc = _pool(p, pool_w[l].astype(BF16), row(pool_scale[l]), colblk=pool_colblk, tb=512)
        xf = _outproj(xf, ya.reshape(n, -1), yb.reshape(n, -1), yc.reshape(n, -1), wa[l], wb[l], wc[l], tm=512)
        xf = _ffn(xf, row(norm_ffn2[l]), ffn2_w_gate[l].astype(BF16), ffn2_w_up[l].astype(BF16),
                  ffn2_w_down[l].astype(BF16), row(norm_final), final_norm=(l == depth - 1), tm=512, tf=512)
    return xf.reshape(bsz, seq, d)
```

```python
import functools

import numpy as np
import jax
import jax.numpy as jnp
from jax import lax
from jax.experimental import pallas as pl
from jax.experimental.pallas import tpu as pltpu

HEAD_DIM = 128
POOL_WINDOWS = (2, 4, 8, 16)
POOL_MAX = max(POOL_WINDOWS)
CONV_WIDTH = 4
NORM_EPS = 1e-6
L2_EPS = 1e-6

LANES = 128
SUBLANES = 8
VMEM_LIMIT_BYTES = 56 * 1024 * 1024

CHUNK = 128
N_LEVELS = 7

F32 = jnp.float32
BF16 = jnp.bfloat16


def _params(sem):
    return pltpu.CompilerParams(dimension_semantics=sem, vmem_limit_bytes=VMEM_LIMIT_BYTES)


def _rms(x, w):
    return x * lax.rsqrt(jnp.mean(x * x, axis=-1, keepdims=True) + NORM_EPS) * w


def _silu(x):
    return x * jax.nn.sigmoid(x)


def _dot(a, b):
    return jnp.dot(a.astype(BF16), b.astype(BF16), preferred_element_type=F32)


def _dot_nt(a, b):
    return lax.dot_general(a.astype(BF16), b.astype(BF16), (((1,), (1,)), ((), ())),
                           preferred_element_type=F32)


def _dot_tn(a, b):
    return lax.dot_general(a.astype(BF16), b.astype(BF16), (((0,), (0,)), ((), ())),
                           preferred_element_type=F32)


def _dot_f32(a, b):
    return jnp.dot(a, b, preferred_element_type=F32, precision=lax.Precision.HIGHEST)


def _pair_level(tt, uu):
    x = tt ^ uu
    hb = jnp.zeros(x.shape, jnp.int32)
    for k in range(1, N_LEVELS):
        hb = hb + (x >= (1 << k)).astype(jnp.int32)
    return jnp.where(tt > uu, hb, -1)


def _split3(x):
    hi = x.astype(BF16)
    r = x - hi.astype(F32)
    mid = r.astype(BF16)
    lo = (r - mid.astype(F32)).astype(BF16)
    return jnp.concatenate([hi, mid, lo], axis=0)


def _ffn_kernel(x_ref, nw_ref, wg_ref, wu_ref, wd_ref, fw_ref, o_ref, h_ref, *, final_norm, sub):
    f = pl.program_id(1)
    tm = x_ref.shape[0]

    @pl.when(f == 0)
    def _():
        x = x_ref[...]
        h_ref[...] = _rms(x, nw_ref[...]).astype(BF16)
        o_ref[...] = x

    for r in range(tm // sub):
        rows = slice(r * sub, (r + 1) * sub)
        h = h_ref[rows, :]
        g = jnp.dot(h, wg_ref[...], preferred_element_type=F32)
        u = jnp.dot(h, wu_ref[...], preferred_element_type=F32)
        a = (_silu(g) * (0.5 * u)).astype(BF16)
        o_ref[rows, :] += jnp.dot(a, wd_ref[...], preferred_element_type=F32)

    if final_norm:
        @pl.when(f == pl.num_programs(1) - 1)
        def _():
            o_ref[...] = _rms(o_ref[...], fw_ref[...])


def _ffn(x, nw, wg, wu, wd, fw, *, final_norm, tm, tf, sub):
    n, d = x.shape
    dff = wg.shape[1]
    tm = min(tm, n)
    sub = min(sub, tm)
    return pl.pallas_call(
        functools.partial(_ffn_kernel, final_norm=final_norm, sub=sub),
        out_shape=jax.ShapeDtypeStruct((n, d), F32),
        grid=(n // tm, dff // tf),
        in_specs=[
            pl.BlockSpec((tm, d), lambda i, f: (i, 0), pipeline_mode=pl.Buffered(1)),
            pl.BlockSpec((1, d), lambda i, f: (0, 0)),
            pl.BlockSpec((d, tf), lambda i, f: (0, f)),
            pl.BlockSpec((d, tf), lambda i, f: (0, f)),
            pl.BlockSpec((tf, d), lambda i, f: (f, 0)),
            pl.BlockSpec((1, d), lambda i, f: (0, 0)),
        ],
        out_specs=pl.BlockSpec((tm, d), lambda i, f: (i, 0)),
        scratch_shapes=[pltpu.VMEM((tm, d), BF16)],
        compiler_params=_params(("parallel", "arbitrary")),
        name="ffn",
    )(x, nw, wg, wu, wd, fw)


def _inproj_kernel(x_ref, nw_ref, w_ref, o_ref, h_ref):
    @pl.when(pl.program_id(1) == 0)
    def _():
        h_ref[...] = _rms(x_ref[...], nw_ref[...]).astype(BF16)

    o_ref[...] = jnp.dot(h_ref[...], w_ref[...], preferred_element_type=F32)


def _inproj(x, nw, w, *, tm, tn):
    n, d = x.shape
    e = w.shape[1]
    tm = min(tm, n)
    return pl.pallas_call(
        _inproj_kernel,
        out_shape=jax.ShapeDtypeStruct((n, e), F32),
        grid=(n // tm, e // tn),
        in_specs=[
            pl.BlockSpec((tm, d), lambda i, j: (i, 0)),
            pl.BlockSpec((1, d), lambda i, j: (0, 0)),
            pl.BlockSpec((d, tn), lambda i, j: (0, j)),
        ],
        out_specs=pl.BlockSpec((tm, tn), lambda i, j: (i, j)),
        scratch_shapes=[pltpu.VMEM((tm, d), BF16)],
        compiler_params=_params(("parallel", "arbitrary")),
        name="inproj",
    )(x, nw, w)


def _outproj_kernel(x_ref, ya_ref, yb_ref, yc_ref, wa_ref, wb_ref, wc_ref, o_ref):
    acc = jnp.dot(ya_ref[...], wa_ref[...], preferred_element_type=F32)
    acc += jnp.dot(yb_ref[...], wb_ref[...], preferred_element_type=F32)
    acc += jnp.dot(yc_ref[...], wc_ref[...], preferred_element_type=F32)
    o_ref[...] = x_ref[...] + acc


def _outproj(x, ya, yb, yc, wa, wb, wc, *, tm):
    n, d = x.shape
    tm = min(tm, n)
    row = lambda a: pl.BlockSpec((tm, a.shape[1]), lambda i: (i, 0))
    full = lambda a: pl.BlockSpec(a.shape, lambda i: (0, 0))
    return pl.pallas_call(
        _outproj_kernel,
        out_shape=jax.ShapeDtypeStruct((n, d), F32),
        grid=(n // tm,),
        in_specs=[row(x), row(ya), row(yb), row(yc), full(wa), full(wb), full(wc)],
        out_specs=row(x),
        compiler_params=_params(("parallel",)),
        name="outproj",
    )(x, ya, yb, yc, wa, wb, wc)


def _lb_kernel(x_ref, o_ref):
    depth = x_ref.shape[0]
    x = x_ref[...]
    e = jnp.exp(x - jnp.max(x, axis=0, keepdims=True))
    sm = e / jnp.sum(e, axis=0, keepdims=True)
    run = sm[0:1]
    first = run
    for l in range(depth):
        if l > 0:
            run = run + sm[l:l + 1]
        lb = run - first
        o_ref[l] = jnp.concatenate([jnp.log(lb), jnp.log1p(-lb), 1.0 - lb], axis=0)


def _lb_params(lb_logits):
    depth, w = lb_logits.shape
    return pl.pallas_call(
        _lb_kernel,
        out_shape=jax.ShapeDtypeStruct((depth, 3, w), F32),
        name="hgrn_lower_bounds",
    )(lb_logits)


def _hgrn_decay_matrix():
    c = CHUNK
    t = np.arange(c)[:, None]
    u = np.arange(c)[None, :]
    blocks = [(u <= t), (u > t)]
    for k in range(N_LEVELS - 1, -1, -1):
        s = 1 << k
        ref = (t // (2 * s)) * (2 * s) + s - 1
        odd = (t & s) != 0
        blocks.append(np.where(odd, (u > ref) & (u <= t), (u > t) & (u <= ref)))
    w = np.concatenate(blocks, axis=0).astype(np.float32)
    return np.concatenate([w, w, w], axis=1)


def _hgrn_kernel(q_ref, f_ref, i_ref, g_ref, lb_ref, nw_ref, w_ref, o_ref, st_ref):
    c = CHUNK
    tb = q_ref.shape[0]

    @pl.when(pl.program_id(2) == 0)
    def _():
        st_ref[...] = jnp.zeros_like(st_ref)

    log_lb = lb_ref[0:1, :]
    log_1m = lb_ref[1:2, :]
    one_m = lb_ref[2:3, :]
    nw = nw_ref[...]

    row = lax.broadcasted_iota(jnp.int32, (c, 1), 0)
    tt = lax.broadcasted_iota(jnp.int32, (c, c), 0)
    uu = lax.broadcasted_iota(jnp.int32, (c, c), 1)
    hb = _pair_level(tt, uu)

    def chunk(ci, st):
        r0 = pl.multiple_of(ci * c, c)
        rows = pl.ds(r0, c)
        q = q_ref[rows, :]
        z = f_ref[rows, :]
        v = i_ref[rows, :]
        gate = g_ref[rows, :]

        qh = _silu(q)
        y = log_1m + (jnp.minimum(z, 0.0) - jnp.log1p(jnp.exp(-jnp.abs(z))))
        log_f = jnp.maximum(log_lb, y) + jnp.log1p(jnp.exp(-jnp.abs(log_lb - y)))
        kk = one_m * jax.nn.sigmoid(-z)

        dall = jnp.dot(w_ref[...], _split3(log_f), preferred_element_type=F32)
        cum = dall[0:c]
        e_cum = jnp.exp(cum)
        e_last = jnp.exp(dall[c:2 * c])

        p = jnp.zeros((c, c), F32)
        for li in range(N_LEVELS):
            k = N_LEVELS - 1 - li
            e = jnp.exp(dall[(2 + li) * c:(3 + li) * c])
            odd = (row & (1 << k)) != 0
            xs = jnp.where(odd, qh, kk) * e
            ps = _dot_nt(jnp.where(odd, xs, 0.0), jnp.where(odd, 0.0, xs))
            p = p + jnp.where(hb == k, ps, 0.0)

        diag = jnp.sum(qh * kk, axis=-1, keepdims=True)
        o = _dot_nt(qh * e_cum, st) + _dot(p, v) + diag * v
        st = st * jnp.exp(cum[c - 1:c, :]) + _dot_tn(v, kk * e_last)

        o_ref[rows, :] = (_rms(o, nw) * _silu(gate)).astype(o_ref.dtype)
        return st

    st_ref[...] = lax.fori_loop(0, tb // c, chunk, st_ref[...])


def _hgrn(p, lbp, nw, wmat, *, heads, tb):
    b, t, _ = p.shape
    tb = min(tb, t)
    col = lambda off: pl.BlockSpec((None, tb, LANES), lambda bi, h, ti: (bi, ti, off + h))
    return pl.pallas_call(
        _hgrn_kernel,
        out_shape=jax.ShapeDtypeStruct((b, t, heads * HEAD_DIM), BF16),
        grid=(b, heads, t // tb),
        in_specs=[
            col(0), col(heads), col(2 * heads), col(3 * heads),
            pl.BlockSpec((3, LANES), lambda bi, h, ti: (0, h)),
            pl.BlockSpec((1, LANES), lambda bi, h, ti: (0, 0)),
            pl.BlockSpec(wmat.shape, lambda bi, h, ti: (0, 0)),
        ],
        out_specs=pl.BlockSpec((None, tb, LANES), lambda bi, h, ti: (bi, ti, h)),
        scratch_shapes=[pltpu.VMEM((HEAD_DIM, HEAD_DIM), F32)],
        compiler_params=_params(("parallel", "parallel", "arbitrary")),
        name="hgrn2",
    )(p, p, p, p, lbp, nw, wmat)


def _gdn_kernel(q_ref, k_ref, v_ref, g_ref, gm_ref, cq_ref, ck_ref, cv_ref, ap_ref, nw_ref, tri_ref,
                o_ref, s_ref, halo_ref, ext_ref, qn_ref, kn_ref, vn_ref,
                u_ref, wq_ref, kd_ref, at_ref, gl_ref, *, head_off):
    c = CHUNK
    tb = q_ref.shape[0]
    h = pl.program_id(1)
    pad = SUBLANES

    @pl.when(pl.program_id(2) == 0)
    def _():
        s_ref[...] = jnp.zeros_like(s_ref)
        halo_ref[...] = jnp.zeros_like(halo_ref)

    def conv(idx, src_ref, cw_ref):
        ext_ref[0:pad, :] = halo_ref[idx]
        ext_ref[pad:pad + tb, :] = src_ref[...]
        halo_ref[idx] = src_ref[tb - pad:tb, :]
        acc = cw_ref[CONV_WIDTH - 1:CONV_WIDTH, :] * ext_ref[pad:pad + tb, :]
        for j in range(CONV_WIDTH - 1):
            sh = CONV_WIDTH - 1 - j
            acc = acc + cw_ref[j:j + 1, :] * ext_ref[pad - sh:pad - sh + tb, :]
        return _silu(acc)

    def l2n(a):
        return a * lax.rsqrt(jnp.sum(a * a, axis=-1, keepdims=True) + L2_EPS)

    qn_ref[...] = l2n(conv(0, q_ref, cq_ref)) * (HEAD_DIM ** -0.5)
    kn_ref[...] = l2n(conv(1, k_ref, ck_ref))
    vn_ref[...] = conv(2, v_ref, cv_ref)

    neg_a = -jnp.exp(ap_ref[0:1, :])
    dt_b = ap_ref[1:2, :]
    nw = nw_ref[...]
    lane = lax.broadcasted_iota(jnp.int32, (1, LANES), 1)
    oh_beta = (lane == h).astype(F32)
    oh_g = (lane == h + head_off).astype(F32)
    tt = lax.broadcasted_iota(jnp.int32, (c, c), 0)
    uu = lax.broadcasted_iota(jnp.int32, (c, c), 1)
    hb = _pair_level(tt, uu)

    chunks = range(tb // c)
    rows_of = lambda ci: slice(ci * c, (ci + 1) * c)
    nbs, rhss = [], []
    for ci in chunks:
        rows = rows_of(ci)
        q = qn_ref[rows, :]
        k = kn_ref[rows, :]
        v = vn_ref[rows, :]
        gm = gm_ref[rows, :]

        xg = gm + dt_b
        g_all = neg_a * (jnp.maximum(xg, 0.0) + jnp.log1p(jnp.exp(-jnp.abs(xg))))
        gc_all = jnp.dot(tri_ref[...], _split3(g_all), preferred_element_type=F32)
        beta = jnp.sum(jax.nn.sigmoid(gm) * oh_beta, axis=-1, keepdims=True)
        gcum = jnp.sum(gc_all * oh_g, axis=-1, keepdims=True)
        g_last = gcum[c - 1:c, :]

        gb = jnp.broadcast_to(gcum, (c, c))
        decay = jnp.exp(jnp.where(tt >= uu, gb - gb.T, -jnp.inf))
        kb = k * beta
        m = _dot_nt(jnp.concatenate([kb, q], axis=0), k)
        nbs.append((-(m[0:c] * decay)).astype(BF16))

        e_g = jnp.exp(gcum)
        rhss.append(jnp.concatenate([v * beta, kb * e_g], axis=1))
        wq_ref[ci, c:2 * c, :] = (q * e_g).astype(BF16)
        kd_ref[rows, :] = (k * jnp.exp(g_last - gcum)).astype(BF16)
        at_ref[rows, :] = (m[c:2 * c] * decay).astype(BF16)
        gl_ref[ci * SUBLANES:(ci + 1) * SUBLANES, :] = jnp.broadcast_to(jnp.exp(g_last), (SUBLANES, LANES))

    mis = [jnp.where(hb == 0, nb, 0).astype(F32) for nb in nbs]
    for lv in range(1, N_LEVELS):
        o_lvs = [jnp.where(hb == lv, nb, 0) for nb in nbs]
        ys = [o_lv.astype(F32) + _dot(mi, o_lv) for mi, o_lv in zip(mis, o_lvs)]
        mis = [mi + y + _dot(y, mi) for mi, y in zip(mis, ys)]

    for ci in chunks:
        sol = rhss[ci] + _dot(mis[ci], rhss[ci])
        u_ref[rows_of(ci), :] = sol[:, 0:HEAD_DIM]
        wq_ref[ci, 0:c, :] = sol[:, HEAD_DIM:2 * HEAD_DIM].astype(BF16)

    def step(ci, s):
        rows = pl.ds(pl.multiple_of(ci * c, c), c)
        ws = jnp.dot(wq_ref[ci], s.astype(BF16), preferred_element_type=F32)
        v_new = (u_ref[rows, :] - ws[0:c]).astype(BF16)
        o = ws[c:2 * c] + jnp.dot(at_ref[rows, :], v_new, preferred_element_type=F32)
        last = gl_ref[pl.ds(pl.multiple_of(ci * SUBLANES, SUBLANES), SUBLANES), :][0:1, :]
        s = s * last + _dot_tn(kd_ref[rows, :], v_new)
        o_ref[rows, :] = (_rms(o, nw) * _silu(g_ref[rows, :])).astype(o_ref.dtype)
        return s

    s_ref[...] = lax.fori_loop(0, tb // c, step, s_ref[...])


def _gdn(p, conv_w, ap, nw, tri, *, heads, col0, gm_col, tb):
    b, t, _ = p.shape
    tb = min(tb, t)
    col = lambda off: pl.BlockSpec((None, tb, LANES), lambda bi, h, ti: (bi, ti, off + h))
    cw = lambda off: pl.BlockSpec((CONV_WIDTH, LANES), lambda bi, h, ti: (0, off + h))
    const = lambda a: pl.BlockSpec(a.shape, lambda bi, h, ti: (0,) * a.ndim)
    return pl.pallas_call(
        functools.partial(_gdn_kernel, head_off=heads),
        out_shape=jax.ShapeDtypeStruct((b, t, heads * HEAD_DIM), BF16),
        grid=(b, heads, t // tb),
        in_specs=[
            col(col0), col(col0 + heads), col(col0 + 2 * heads), col(col0 + 3 * heads),
            pl.BlockSpec((None, tb, LANES), lambda bi, h, ti: (bi, ti, gm_col)),
            cw(0), cw(heads), cw(2 * heads),
            const(ap), const(nw), const(tri),
        ],
        out_specs=pl.BlockSpec((None, tb, LANES), lambda bi, h, ti: (bi, ti, h)),
        scratch_shapes=[
            pltpu.VMEM((HEAD_DIM, HEAD_DIM), F32),
            pltpu.VMEM((3, SUBLANES, LANES), F32),
            pltpu.VMEM((tb + SUBLANES, LANES), F32),
            pltpu.VMEM((tb, LANES), F32),
            pltpu.VMEM((tb, LANES), F32),
            pltpu.VMEM((tb, LANES), F32),
            pltpu.VMEM((tb, HEAD_DIM), F32),
            pltpu.VMEM((tb // CHUNK, 2 * CHUNK, HEAD_DIM), BF16),
            pltpu.VMEM((tb, HEAD_DIM), BF16),
            pltpu.VMEM((tb, CHUNK), BF16),
            pltpu.VMEM((tb // CHUNK * SUBLANES, LANES), F32),
        ],
        compiler_params=_params(("parallel", "parallel", "arbitrary")),
        name="gated_deltanet",
    )(p, p, p, p, p, conv_w, conv_w, conv_w, ap, nw, tri)


def _pool_kernel(u_ref, w_ref, sc_ref, o_ref, ext_ref):
    tb = u_ref.shape[0]
    ti = pl.program_id(1)

    @pl.when(ti == 0)
    def _():
        ext_ref[0:POOL_MAX, :] = jnp.zeros((POOL_MAX, ext_ref.shape[1]), F32)

    ext_ref[POOL_MAX:POOL_MAX + tb, :] = u_ref[...]
    pos = ti * tb + lax.broadcasted_iota(jnp.int32, (tb, 1), 0)
    for gi, win in enumerate(POOL_WINDOWS):
        sl = slice(gi * LANES, (gi + 1) * LANES)
        cur = ext_ref[POOL_MAX:POOL_MAX + tb, sl]
        ws = cur
        for j in range(1, win):
            ws = ws + ext_ref[POOL_MAX - j:POOL_MAX - j + tb, sl]
        count = jnp.minimum(pos + 1, win).astype(F32)
        m = ws / count - cur
        o_ref[:, sl] = (_dot(m, w_ref[gi]) * sc_ref[:, sl]).astype(o_ref.dtype)
    ext_ref[0:POOL_MAX, :] = ext_ref[tb:tb + POOL_MAX, :]


def _pool(p, pool_w, scale, *, colblk, tb):
    b, t, _ = p.shape
    tb = min(tb, t)
    width = len(POOL_WINDOWS) * LANES
    return pl.pallas_call(
        _pool_kernel,
        out_shape=jax.ShapeDtypeStruct((b, t, width), BF16),
        grid=(b, t // tb),
        in_specs=[
            pl.BlockSpec((None, tb, width), lambda bi, ti: (bi, ti, colblk)),
            pl.BlockSpec(pool_w.shape, lambda bi, ti: (0, 0, 0)),
            pl.BlockSpec((1, width), lambda bi, ti: (0, 0)),
        ],
        out_specs=pl.BlockSpec((None, tb, width), lambda bi, ti: (bi, ti, 0)),
        scratch_shapes=[pltpu.VMEM((tb + POOL_MAX, width), F32)],
        compiler_params=_params(("parallel", "arbitrary")),
        name="causal_pool",
    )(p, pool_w, scale)


def kernel(x, lb_logits, norm_ffn1, ffn1_w_gate, ffn1_w_up, ffn1_w_down, norm_mix, w_in, gdn_conv_w, gdn_a_log, gdn_dt_bias, hgrn_norm_w, gdn_norm_w, pool_w, pool_scale, w_out, norm_ffn2, ffn2_w_gate, ffn2_w_up, ffn2_w_down, norm_final):
    bsz, seq, d = x.shape
    depth = lb_logits.shape[0]
    hgrn_w = lb_logits.shape[1]
    gdn_heads = gdn_a_log.shape[1]
    gdn_w = gdn_heads * HEAD_DIM
    pool_width = pool_scale.shape[1]
    hgrn_heads = hgrn_w // HEAD_DIM
    n = bsz * seq

    main = 4 * hgrn_w + 4 * gdn_w
    e_used = main + pool_width + LANES
    tn_in = 1152
    e_pad = -(-e_used // tn_in) * tn_in
    scal = w_in[:, :, main:main + 2 * gdn_heads]
    w_in_r = jnp.concatenate(
        [w_in[:, :, :main], w_in[:, :, main + 2 * gdn_heads:], scal,
         jnp.zeros((depth, d, e_pad - e_used + LANES - 2 * gdn_heads), w_in.dtype)], axis=-1).astype(BF16)
    hgrn_col0 = 0
    gdn_col0 = 4 * hgrn_heads
    pool_colblk = main // pool_width
    gm_col = (main + pool_width) // LANES

    lane_pad = lambda a: jnp.pad(a, ((0, 0), (gdn_heads, LANES - 2 * gdn_heads)))
    gdn_ap = jnp.stack([lane_pad(gdn_a_log), lane_pad(gdn_dt_bias)], axis=1)

    hgrn_wmat = jnp.asarray(_hgrn_decay_matrix(), BF16)
    tri = np.tril(np.ones((CHUNK, CHUNK), np.float32))
    tri3 = jnp.asarray(np.concatenate([tri, tri, tri], axis=1), BF16)

    lbp = _lb_params(lb_logits.astype(F32))

    wa = w_out[:, :hgrn_w].astype(BF16)
    wb = w_out[:, hgrn_w:hgrn_w + gdn_w].astype(BF16)
    wc = w_out[:, hgrn_w + gdn_w:].astype(BF16)

    xf = x.reshape(n, d)
    row = lambda a: a.reshape(1, -1)
    for l in range(depth):
        xf = _ffn(xf, row(norm_ffn1[l]), ffn1_w_gate[l].astype(BF16), ffn1_w_up[l].astype(BF16),
                  ffn1_w_down[l].astype(BF16), row(norm_final), final_norm=False, tm=1024, tf=512, sub=512)
        p = _inproj(xf, row(norm_mix[l]), w_in_r[l], tm=1024, tn=tn_in).reshape(bsz, seq, e_pad)
        ya = _hgrn(p, lbp[l], row(hgrn_norm_w[l]), hgrn_wmat, heads=hgrn_heads, tb=512)
        yb = _gdn(p, gdn_conv_w[l], gdn_ap[l], row(gdn_norm_w[l]), tri3,
                  heads=gdn_heads, col0=gdn_col0, gm_col=gm_col, tb=1024)
        yc = _pool(p, pool_w[l].astype(BF16), row(pool_scale[l]), colblk=pool_colblk, tb=512)
        xf = _outproj(xf, ya.reshape(n, -1), yb.reshape(n, -1), yc.reshape(n, -1), wa[l], wb[l], wc[l], tm=512)
        xf = _ffn(xf, row(norm_ffn2[l]), ffn2_w_gate[l].astype(BF16), ffn2_w_up[l].astype(BF16),
                  ffn2_w_down[l].astype(BF16), row(norm_final), final_norm=(l == depth - 1), tm=1024, tf=512, sub=512)
    return xf.reshape(bsz, seq, d)
```

```python
import functools

import numpy as np
import jax
import jax.numpy as jnp
from jax import lax
from jax.experimental import pallas as pl
from jax.experimental.pallas import tpu as pltpu

HEAD_DIM = 128
POOL_WINDOWS = (2, 4, 8, 16)
POOL_MAX = max(POOL_WINDOWS)
CONV_WIDTH = 4
NORM_EPS = 1e-6
L2_EPS = 1e-6

LANES = 128
SUBLANES = 8
VMEM_LIMIT_BYTES = 56 * 1024 * 1024

CHUNK = 128
N_LEVELS = 7

F32 = jnp.float32
BF16 = jnp.bfloat16


def _params(sem):
    return pltpu.CompilerParams(dimension_semantics=sem, vmem_limit_bytes=VMEM_LIMIT_BYTES)


def _rms(x, w):
    return x * lax.rsqrt(jnp.mean(x * x, axis=-1, keepdims=True) + NORM_EPS) * w


def _silu(x):
    return x * jax.nn.sigmoid(x)


def _log1p_exp(x):
    return jnp.log(1.0 + jnp.exp(x))


def _dot(a, b):
    return jnp.dot(a.astype(BF16), b.astype(BF16), preferred_element_type=F32)


def _dot_nt(a, b):
    return lax.dot_general(a.astype(BF16), b.astype(BF16), (((1,), (1,)), ((), ())),
                           preferred_element_type=F32)


def _dot_tn(a, b):
    return lax.dot_general(a.astype(BF16), b.astype(BF16), (((0,), (0,)), ((), ())),
                           preferred_element_type=F32)


def _dot_f32(a, b):
    return jnp.dot(a, b, preferred_element_type=F32, precision=lax.Precision.HIGHEST)


def _pair_level(tt, uu):
    x = tt ^ uu
    hb = jnp.zeros(x.shape, jnp.int32)
    for k in range(1, N_LEVELS):
        hb = hb + (x >= (1 << k)).astype(jnp.int32)
    return jnp.where(tt > uu, hb, -1)


def _split3(x):
    hi = x.astype(BF16)
    r = x - hi.astype(F32)
    mid = r.astype(BF16)
    lo = (r - mid.astype(F32)).astype(BF16)
    return jnp.concatenate([hi, mid, lo], axis=0)


def _ffn_kernel(x_ref, nw_ref, wg_ref, wu_ref, wd_ref, fw_ref, o_ref, h_ref, *, final_norm, sub):
    f = pl.program_id(1)
    tm = x_ref.shape[0]

    @pl.when(f == 0)
    def _():
        x = x_ref[...]
        h_ref[...] = _rms(x, nw_ref[...]).astype(BF16)
        o_ref[...] = x

    for r in range(tm // sub):
        rows = slice(r * sub, (r + 1) * sub)
        h = h_ref[rows, :]
        g = jnp.dot(h, wg_ref[...], preferred_element_type=F32)
        u = jnp.dot(h, wu_ref[...], preferred_element_type=F32)
        a = (_silu(g) * (0.5 * u)).astype(BF16)
        o_ref[rows, :] += jnp.dot(a, wd_ref[...], preferred_element_type=F32)

    if final_norm:
        @pl.when(f == pl.num_programs(1) - 1)
        def _():
            o_ref[...] = _rms(o_ref[...], fw_ref[...])


def _ffn(x, nw, wg, wu, wd, fw, *, final_norm, tm, tf, sub):
    n, d = x.shape
    dff = wg.shape[1]
    tm = min(tm, n)
    sub = min(sub, tm)
    return pl.pallas_call(
        functools.partial(_ffn_kernel, final_norm=final_norm, sub=sub),
        out_shape=jax.ShapeDtypeStruct((n, d), F32),
        grid=(n // tm, dff // tf),
        in_specs=[
            pl.BlockSpec((tm, d), lambda i, f: (i, 0), pipeline_mode=pl.Buffered(1)),
            pl.BlockSpec((1, d), lambda i, f: (0, 0)),
            pl.BlockSpec((d, tf), lambda i, f: (0, f)),
            pl.BlockSpec((d, tf), lambda i, f: (0, f)),
            pl.BlockSpec((tf, d), lambda i, f: (f, 0)),
            pl.BlockSpec((1, d), lambda i, f: (0, 0)),
        ],
        out_specs=pl.BlockSpec((tm, d), lambda i, f: (i, 0)),
        scratch_shapes=[pltpu.VMEM((tm, d), BF16)],
        compiler_params=_params(("parallel", "arbitrary")),
        name="ffn",
    )(x, nw, wg, wu, wd, fw)


def _inproj_kernel(x_ref, nw_ref, w_ref, o_ref, h_ref):
    @pl.when(pl.program_id(1) == 0)
    def _():
        h_ref[...] = _rms(x_ref[...], nw_ref[...]).astype(BF16)

    o_ref[...] = jnp.dot(h_ref[...], w_ref[...], preferred_element_type=F32)


def _inproj(x, nw, w, *, tm, tn):
    n, d = x.shape
    e = w.shape[1]
    tm = min(tm, n)
    return pl.pallas_call(
        _inproj_kernel,
        out_shape=jax.ShapeDtypeStruct((n, e), F32),
        grid=(n // tm, e // tn),
        in_specs=[
            pl.BlockSpec((tm, d), lambda i, j: (i, 0)),
            pl.BlockSpec((1, d), lambda i, j: (0, 0)),
            pl.BlockSpec((d, tn), lambda i, j: (0, j)),
        ],
        out_specs=pl.BlockSpec((tm, tn), lambda i, j: (i, j)),
        scratch_shapes=[pltpu.VMEM((tm, d), BF16)],
        compiler_params=_params(("parallel", "arbitrary")),
        name="inproj",
    )(x, nw, w)


def _outproj_kernel(x_ref, ya_ref, yb_ref, yc_ref, w_ref, o_ref):
    na, nb = ya_ref.shape[1], yb_ref.shape[1]
    acc = jnp.dot(ya_ref[...], w_ref[0:na, :], preferred_element_type=F32)
    acc += jnp.dot(yb_ref[...], w_ref[na:na + nb, :], preferred_element_type=F32)
    acc += jnp.dot(yc_ref[...], w_ref[na + nb:, :], preferred_element_type=F32)
    o_ref[...] = x_ref[...] + acc


def _outproj(x, ya, yb, yc, w, *, tm):
    n, d = x.shape
    tm = min(tm, n)
    row = lambda a: pl.BlockSpec((tm, a.shape[1]), lambda i: (i, 0))
    return pl.pallas_call(
        _outproj_kernel,
        out_shape=jax.ShapeDtypeStruct((n, d), F32),
        grid=(n // tm,),
        in_specs=[row(x), row(ya), row(yb), row(yc), pl.BlockSpec(w.shape, lambda i: (0, 0))],
        out_specs=row(x),
        compiler_params=_params(("parallel",)),
        name="outproj",
    )(x, ya, yb, yc, w)


def _lb_kernel(x_ref, o_ref):
    depth = x_ref.shape[0]
    x = x_ref[...]
    e = jnp.exp(x - jnp.max(x, axis=0, keepdims=True))
    sm = e / jnp.sum(e, axis=0, keepdims=True)
    run = sm[0:1]
    first = run
    for l in range(depth):
        if l > 0:
            run = run + sm[l:l + 1]
        lb = run - first
        o_ref[l] = jnp.concatenate([jnp.log(lb), jnp.log1p(-lb), 1.0 - lb], axis=0)


def _lb_params(lb_logits):
    depth, w = lb_logits.shape
    return pl.pallas_call(
        _lb_kernel,
        out_shape=jax.ShapeDtypeStruct((depth, 3, w), F32),
        name="hgrn_lower_bounds",
    )(lb_logits)


def _hgrn_decay_matrix():
    c = CHUNK
    t = np.arange(c)[:, None]
    u = np.arange(c)[None, :]
    blocks = [(u <= t)]
    for k in range(N_LEVELS - 1, 0, -1):
        s = 1 << k
        ref = (t // (2 * s)) * (2 * s) + s - 1
        odd = (t & s) != 0
        blocks.append(np.where(odd, (u > ref) & (u <= t), (u > t) & (u <= ref)))
    w = np.concatenate(blocks, axis=0).astype(np.float32)
    return np.concatenate([w, w], axis=1)


def _hgrn_kernel(q_ref, f_ref, i_ref, g_ref, lb_ref, nw_ref, w_ref, o_ref,
                 st_ref, oraw_ref, qc_ref, r_ref, el_ref):
    c = CHUNK
    tb = q_ref.shape[0]

    @pl.when(pl.program_id(2) == 0)
    def _():
        st_ref[...] = jnp.zeros_like(st_ref)

    log_lb = lb_ref[0:1, :]
    log_1m = lb_ref[1:2, :]
    one_m = lb_ref[2:3, :]
    nw = nw_ref[...]

    row = lax.broadcasted_iota(jnp.int32, (c, 1), 0)
    tt = lax.broadcasted_iota(jnp.int32, (c, c), 0)
    uu = lax.broadcasted_iota(jnp.int32, (c, c), 1)
    hb = _pair_level(tt, uu)
    odd_of = lambda k: (row & (1 << k)) != 0

    n_chunks = tb // c
    assert n_chunks % 2 == 0
    rows_of = lambda ci: slice(ci * c, (ci + 1) * c)
    qhs, kks, lfs, his, los = [], [], [], [], []
    for ci in range(n_chunks):
        rows = rows_of(ci)
        z = f_ref[rows, :]
        y = log_1m + (jnp.minimum(z, 0.0) - _log1p_exp(-jnp.abs(z)))
        log_f = jnp.maximum(log_lb, y) + _log1p_exp(-jnp.abs(log_lb - y))
        hi = log_f.astype(BF16)
        qhs.append(_silu(q_ref[rows, :]))
        kks.append(one_m * jax.nn.sigmoid(-z))
        lfs.append(log_f)
        his.append(hi)
        los.append((log_f - hi.astype(F32)).astype(BF16))

    dalls = []
    for a in range(0, n_chunks, 2):
        rhs = jnp.concatenate([jnp.concatenate([his[a], his[a + 1]], axis=1),
                               jnp.concatenate([los[a], los[a + 1]], axis=1)], axis=0)
        d = jnp.dot(w_ref[...], rhs, preferred_element_type=F32)
        dalls += [d[:, 0:HEAD_DIM], d[:, HEAD_DIM:2 * HEAD_DIM]]

    ps = [jnp.zeros((c, c), F32) for _ in range(n_chunks)]
    for k in range(N_LEVELS - 1, -1, -1):
        odd = odd_of(k)
        for ci in range(n_chunks):
            li = N_LEVELS - 1 - k
            expo = dalls[ci][(1 + li) * c:(2 + li) * c] if k > 0 else jnp.where(odd, lfs[ci], 0.0)
            xs = jnp.where(odd, qhs[ci], kks[ci]) * jnp.exp(expo)
            pk = _dot_nt(jnp.where(odd, xs, 0.0), jnp.where(odd, 0.0, xs))
            ps[ci] = ps[ci] + jnp.where(hb == k, pk, 0.0)

    for ci in range(n_chunks):
        rows = rows_of(ci)
        v = i_ref[rows, :]
        cum = dalls[ci][0:c]
        last = cum[c - 1:c, :]
        diag = jnp.sum(qhs[ci] * kks[ci], axis=-1, keepdims=True)
        oraw_ref[rows, :] = _dot(ps[ci], v) + diag * v
        qc_ref[rows, :] = (qhs[ci] * jnp.exp(cum)).astype(BF16)
        r_ref[rows, :] = _dot_tn(v, kks[ci] * jnp.exp(last - cum))
        el_ref[ci * SUBLANES:(ci + 1) * SUBLANES, :] = jnp.broadcast_to(jnp.exp(last), (SUBLANES, LANES))

    st = st_ref[...]
    for ci in range(n_chunks):
        rows = rows_of(ci)
        oraw_ref[rows, :] += _dot_nt(qc_ref[rows, :], st)
        st = st * el_ref[ci * SUBLANES:ci * SUBLANES + 1, :] + r_ref[rows, :]
    st_ref[...] = st

    o_ref[...] = (_rms(oraw_ref[...], nw) * _silu(g_ref[...])).astype(o_ref.dtype)


def _hgrn(p, lbp, nw, wmat, *, heads, tb):
    b, t, _ = p.shape
    tb = min(tb, t)
    col = lambda off: pl.BlockSpec((None, tb, LANES), lambda bi, h, ti: (bi, ti, off + h))
    return pl.pallas_call(
        _hgrn_kernel,
        out_shape=jax.ShapeDtypeStruct((b, t, heads * HEAD_DIM), BF16),
        grid=(b, heads, t // tb),
        in_specs=[
            col(0), col(heads), col(2 * heads), col(3 * heads),
            pl.BlockSpec((3, LANES), lambda bi, h, ti: (0, h)),
            pl.BlockSpec((1, LANES), lambda bi, h, ti: (0, 0)),
            pl.BlockSpec(wmat.shape, lambda bi, h, ti: (0, 0)),
        ],
        out_specs=pl.BlockSpec((None, tb, LANES), lambda bi, h, ti: (bi, ti, h)),
        scratch_shapes=[
            pltpu.VMEM((HEAD_DIM, HEAD_DIM), F32),
            pltpu.VMEM((tb, HEAD_DIM), F32),
            pltpu.VMEM((tb, HEAD_DIM), BF16),
            pltpu.VMEM((tb, HEAD_DIM), F32),
            pltpu.VMEM((tb // CHUNK * SUBLANES, LANES), F32),
        ],
        compiler_params=_params(("parallel", "parallel", "arbitrary")),
        name="hgrn2",
    )(p, p, p, p, lbp, nw, wmat)


def _gdn_gates_kernel(gm_ref, ap_ref, tri_ref, o_ref, *, heads):
    c = CHUNK
    gm = gm_ref[...]
    xg = gm + ap_ref[1:2, :]
    g_all = -jnp.exp(ap_ref[0:1, :]) * (jnp.maximum(xg, 0.0) + _log1p_exp(-jnp.abs(xg)))
    lane = lax.broadcasted_iota(jnp.int32, (1, LANES), 1)
    beta = jax.nn.sigmoid(gm)
    for ci in range(gm.shape[0] // c):
        rows = slice(ci * c, (ci + 1) * c)
        gcum = jnp.dot(tri_ref[...], _split3(g_all[rows]), preferred_element_type=F32)
        o_ref[rows, :] = jnp.where(lane < heads, beta[rows], gcum)


def _gdn_gates(p, ap, tri, *, heads, gm_col, tb):
    b, t, _ = p.shape
    tb = min(tb, t)
    return pl.pallas_call(
        functools.partial(_gdn_gates_kernel, heads=heads),
        out_shape=jax.ShapeDtypeStruct((b, t, LANES), F32),
        grid=(b, t // tb),
        in_specs=[
            pl.BlockSpec((None, tb, LANES), lambda bi, ti: (bi, ti, gm_col)),
            pl.BlockSpec(ap.shape, lambda bi, ti: (0, 0)),
            pl.BlockSpec(tri.shape, lambda bi, ti: (0, 0)),
        ],
        out_specs=pl.BlockSpec((None, tb, LANES), lambda bi, ti: (bi, ti, 0)),
        compiler_params=_params(("parallel", "parallel")),
        name="gdn_gates",
    )(p, ap, tri)


def _gdn_kernel(q_ref, k_ref, v_ref, g_ref, gm_ref, cq_ref, ck_ref, cv_ref, nw_ref,
                o_ref, s_ref, halo_ref, ext_ref, qn_ref, kn_ref, vn_ref,
                qd_ref, kd_ref, at_ref, gl_ref, r_ref, pq_ref, oraw_ref, *, head_off):
    c = CHUNK
    tb = q_ref.shape[0]
    h = pl.program_id(1)
    pad = SUBLANES

    @pl.when(pl.program_id(2) == 0)
    def _():
        s_ref[...] = jnp.zeros_like(s_ref)
        halo_ref[...] = jnp.zeros_like(halo_ref)

    def conv(idx, src_ref, cw_ref):
        ext_ref[0:pad, :] = halo_ref[idx]
        ext_ref[pad:pad + tb, :] = src_ref[...]
        halo_ref[idx] = src_ref[tb - pad:tb, :]
        acc = cw_ref[CONV_WIDTH - 1:CONV_WIDTH, :] * ext_ref[pad:pad + tb, :]
        for j in range(CONV_WIDTH - 1):
            sh = CONV_WIDTH - 1 - j
            acc = acc + cw_ref[j:j + 1, :] * ext_ref[pad - sh:pad - sh + tb, :]
        return _silu(acc)

    def l2n(a):
        return a * lax.rsqrt(jnp.sum(a * a, axis=-1, keepdims=True) + L2_EPS)

    qn_ref[...] = l2n(conv(0, q_ref, cq_ref)) * (HEAD_DIM ** -0.5)
    kn_ref[...] = l2n(conv(1, k_ref, ck_ref))
    vn_ref[...] = conv(2, v_ref, cv_ref)

    nw = nw_ref[...]
    lane = lax.broadcasted_iota(jnp.int32, (1, LANES), 1)
    oh_beta = (lane == h).astype(F32)
    oh_g = (lane == h + head_off).astype(F32)
    tt = lax.broadcasted_iota(jnp.int32, (c, c), 0)
    uu = lax.broadcasted_iota(jnp.int32, (c, c), 1)
    hb = _pair_level(tt, uu)

    chunks = range(tb // c)
    rows_of = lambda ci: slice(ci * c, (ci + 1) * c)
    nbs, rhss = [], []
    for ci in chunks:
        rows = rows_of(ci)
        q = qn_ref[rows, :]
        k = kn_ref[rows, :]
        v = vn_ref[rows, :]
        gm = gm_ref[rows, :]

        beta = jnp.sum(gm * oh_beta, axis=-1, keepdims=True)
        gcum = jnp.sum(gm * oh_g, axis=-1, keepdims=True)
        g_last = gcum[c - 1:c, :]

        gb = jnp.broadcast_to(gcum, (c, c))
        decay = jnp.exp(jnp.where(tt >= uu, gb - gb.T, -jnp.inf))
        kb = k * beta
        m = _dot_nt(jnp.concatenate([kb, q], axis=0), k)
        nbs.append((-(m[0:c] * decay)).astype(BF16))

        e_g = jnp.exp(gcum)
        rhss.append(jnp.concatenate([v * beta, kb * e_g], axis=1))
        qd_ref[rows, :] = q * e_g
        kd_ref[rows, :] = (k * jnp.exp(g_last - gcum)).astype(BF16)
        at_ref[rows, :] = (m[c:2 * c] * decay).astype(BF16)
        gl_ref[ci * SUBLANES:(ci + 1) * SUBLANES, :] = jnp.broadcast_to(jnp.exp(g_last), (SUBLANES, LANES))

    mis = [jnp.where(hb == 0, nb, 0).astype(F32) for nb in nbs]
    for lv in range(1, N_LEVELS):
        o_lvs = [jnp.where(hb == lv, nb, 0) for nb in nbs]
        ys = [o_lv.astype(F32) + _dot(mi, o_lv) for mi, o_lv in zip(mis, o_lvs)]
        mis = [mi + y + _dot(y, mi) for mi, y in zip(mis, ys)]

    sols = [rhs + _dot(mi, rhs) for mi, rhs in zip(mis, rhss)]
    for ci in chunks:
        rows = rows_of(ci)
        sol = sols[ci].astype(BF16)
        kd_uw = _dot_tn(kd_ref[rows, :], sol)
        at_uw = jnp.dot(at_ref[rows, :], sol, preferred_element_type=F32)
        r_ref[rows, :] = kd_uw[:, 0:HEAD_DIM]
        pq_ref[ci, 0:c, :] = kd_uw[:, HEAD_DIM:2 * HEAD_DIM].astype(BF16)
        pq_ref[ci, c:2 * c, :] = (qd_ref[rows, :] - at_uw[:, HEAD_DIM:2 * HEAD_DIM]).astype(BF16)
        oraw_ref[rows, :] = at_uw[:, 0:HEAD_DIM]

    s = s_ref[...]
    for ci in chunks:
        rows = rows_of(ci)
        z = jnp.dot(pq_ref[ci], s.astype(BF16), preferred_element_type=F32)
        oraw_ref[rows, :] += z[c:2 * c]
        s = s * gl_ref[ci * SUBLANES:ci * SUBLANES + 1, :] - z[0:c] + r_ref[rows, :]
    s_ref[...] = s

    o_ref[...] = (_rms(oraw_ref[...], nw) * _silu(g_ref[...])).astype(o_ref.dtype)


def _gdn(p, gates, conv_w, nw, *, heads, col0, tb):
    b, t, _ = p.shape
    tb = min(tb, t)
    col = lambda off: pl.BlockSpec((None, tb, LANES), lambda bi, h, ti: (bi, ti, off + h))
    cw = lambda off: pl.BlockSpec((CONV_WIDTH, LANES), lambda bi, h, ti: (0, off + h))
    const = lambda a: pl.BlockSpec(a.shape, lambda bi, h, ti: (0,) * a.ndim)
    return pl.pallas_call(
        functools.partial(_gdn_kernel, head_off=heads),
        out_shape=jax.ShapeDtypeStruct((b, t, heads * HEAD_DIM), BF16),
        grid=(b, heads, t // tb),
        in_specs=[
            col(col0), col(col0 + heads), col(col0 + 2 * heads), col(col0 + 3 * heads),
            pl.BlockSpec((None, tb, LANES), lambda bi, h, ti: (bi, ti, 0)),
            cw(0), cw(heads), cw(2 * heads),
            const(nw),
        ],
        out_specs=pl.BlockSpec((None, tb, LANES), lambda bi, h, ti: (bi, ti, h)),
        scratch_shapes=[
            pltpu.VMEM((HEAD_DIM, HEAD_DIM), F32),
            pltpu.VMEM((3, SUBLANES, LANES), F32),
            pltpu.VMEM((tb + SUBLANES, LANES), F32),
            pltpu.VMEM((tb, LANES), F32),
            pltpu.VMEM((tb, LANES), F32),
            pltpu.VMEM((tb, LANES), F32),
            pltpu.VMEM((tb, HEAD_DIM), F32),
            pltpu.VMEM((tb, HEAD_DIM), BF16),
            pltpu.VMEM((tb, CHUNK), BF16),
            pltpu.VMEM((tb // CHUNK * SUBLANES, LANES), F32),
            pltpu.VMEM((tb, HEAD_DIM), F32),
            pltpu.VMEM((tb // CHUNK, 2 * CHUNK, HEAD_DIM), BF16),
            pltpu.VMEM((tb, HEAD_DIM), F32),
        ],
        compiler_params=_params(("parallel", "parallel", "arbitrary")),
        name="gated_deltanet",
    )(p, p, p, p, gates, conv_w, conv_w, conv_w, nw)


def _pool_kernel(u_ref, w_ref, sc_ref, o_ref, ext_ref):
    tb = u_ref.shape[0]
    ti = pl.program_id(1)

    @pl.when(ti == 0)
    def _():
        ext_ref[0:POOL_MAX, :] = jnp.zeros((POOL_MAX, ext_ref.shape[1]), F32)

    ext_ref[POOL_MAX:POOL_MAX + tb, :] = u_ref[...]
    pos = ti * tb + lax.broadcasted_iota(jnp.int32, (tb, 1), 0)
    for gi, win in enumerate(POOL_WINDOWS):
        sl = slice(gi * LANES, (gi + 1) * LANES)
        cur = ext_ref[POOL_MAX:POOL_MAX + tb, sl]
        ws = cur
        for j in range(1, win):
            ws = ws + ext_ref[POOL_MAX - j:POOL_MAX - j + tb, sl]
        count = jnp.minimum(pos + 1, win).astype(F32)
        m = ws / count - cur
        o_ref[:, sl] = (_dot(m, w_ref[gi]) * sc_ref[:, sl]).astype(o_ref.dtype)
    ext_ref[0:POOL_MAX, :] = ext_ref[tb:tb + POOL_MAX, :]


def _pool(p, pool_w, scale, *, colblk, tb):
    b, t, _ = p.shape
    tb = min(tb, t)
    width = len(POOL_WINDOWS) * LANES
    return pl.pallas_call(
        _pool_kernel,
        out_shape=jax.ShapeDtypeStruct((b, t, width), BF16),
        grid=(b, t // tb),
        in_specs=[
            pl.BlockSpec((None, tb, width), lambda bi, ti: (bi, ti, colblk)),
            pl.BlockSpec(pool_w.shape, lambda bi, ti: (0, 0, 0)),
            pl.BlockSpec((1, width), lambda bi, ti: (0, 0)),
        ],
        out_specs=pl.BlockSpec((None, tb, width), lambda bi, ti: (bi, ti, 0)),
        scratch_shapes=[pltpu.VMEM((tb + POOL_MAX, width), F32)],
        compiler_params=_params(("parallel", "arbitrary")),
        name="causal_pool",
    )(p, pool_w, scale)


def kernel(x, lb_logits, norm_ffn1, ffn1_w_gate, ffn1_w_up, ffn1_w_down, norm_mix, w_in, gdn_conv_w, gdn_a_log, gdn_dt_bias, hgrn_norm_w, gdn_norm_w, pool_w, pool_scale, w_out, norm_ffn2, ffn2_w_gate, ffn2_w_up, ffn2_w_down, norm_final):
    bsz, seq, d = x.shape
    depth = lb_logits.shape[0]
    hgrn_w = lb_logits.shape[1]
    gdn_heads = gdn_a_log.shape[1]
    gdn_w = gdn_heads * HEAD_DIM
    pool_width = pool_scale.shape[1]
    hgrn_heads = hgrn_w // HEAD_DIM
    n = bsz * seq

    main = 4 * hgrn_w + 4 * gdn_w
    e_used = main + pool_width + LANES
    tn_in = 768
    e_pad = -(-e_used // tn_in) * tn_in
    w_in_b = w_in.astype(BF16)
    w_in_r = jnp.concatenate(
        [w_in_b[:, :, :main], w_in_b[:, :, main + 2 * gdn_heads:], w_in_b[:, :, main:main + 2 * gdn_heads],
         jnp.zeros((depth, d, e_pad - e_used + LANES - 2 * gdn_heads), BF16)], axis=-1)
    gdn_col0 = 4 * hgrn_heads
    pool_colblk = main // pool_width
    gm_col = (main + pool_width) // LANES

    lane_pad = lambda a: jnp.pad(a, ((0, 0), (gdn_heads, LANES - 2 * gdn_heads)))
    gdn_ap = jnp.stack([lane_pad(gdn_a_log), lane_pad(gdn_dt_bias)], axis=1)

    hgrn_wmat = jnp.asarray(_hgrn_decay_matrix(), BF16)
    tri = np.tril(np.ones((CHUNK, CHUNK), np.float32))
    tri3 = jnp.asarray(np.concatenate([tri, tri, tri], axis=1), BF16)

    lbp = _lb_params(lb_logits.astype(F32))

    w_out_b = w_out.astype(BF16)

    xf = x.reshape(n, d)
    row = lambda a: a.reshape(1, -1)
    for l in range(depth):
        xf = _ffn(xf, row(norm_ffn1[l]), ffn1_w_gate[l].astype(BF16), ffn1_w_up[l].astype(BF16),
                  ffn1_w_down[l].astype(BF16), row(norm_final), final_norm=False, tm=1024, tf=512, sub=512)
        p = _inproj(xf, row(norm_mix[l]), w_in_r[l], tm=1024, tn=tn_in).reshape(bsz, seq, e_pad)
        ya = _hgrn(p, lbp[l], row(hgrn_norm_w[l]), hgrn_wmat, heads=hgrn_heads, tb=1024)
        gates = _gdn_gates(p, gdn_ap[l], tri3, heads=gdn_heads, gm_col=gm_col, tb=1024)
        yb = _gdn(p, gates, gdn_conv_w[l], row(gdn_norm_w[l]), heads=gdn_heads, col0=gdn_col0, tb=1024)
        yc = _pool(p, pool_w[l].astype(BF16), row(pool_scale[l]), colblk=pool_colblk, tb=512)
        xf = _outproj(xf, ya.reshape(n, -1), yb.reshape(n, -1), yc.reshape(n, -1), w_out_b[l], tm=512)
        xf = _ffn(xf, row(norm_ffn2[l]), ffn2_w_gate[l].astype(BF16), ffn2_w_up[l].astype(BF16),
                  ffn2_w_down[l].astype(BF16), row(norm_final), final_norm=(l == depth - 1), tm=1024, tf=512, sub=512)
    return xf.reshape(bsz, seq, d)
```

```python
import functools

import numpy as np
import jax
import jax.numpy as jnp
from jax import lax
from jax.experimental import pallas as pl
from jax.experimental.pallas import tpu as pltpu

HEAD_DIM = 128
POOL_WINDOWS = (2, 4, 8, 16)
POOL_MAX = max(POOL_WINDOWS)
CONV_WIDTH = 4
NORM_EPS = 1e-6
L2_EPS = 1e-6

LANES = 128
SUBLANES = 8
VMEM_LIMIT_BYTES = 56 * 1024 * 1024

CHUNK = 128
N_LEVELS = 7

F32 = jnp.float32
BF16 = jnp.bfloat16


def _params(sem):
    return pltpu.CompilerParams(dimension_semantics=sem, vmem_limit_bytes=VMEM_LIMIT_BYTES)


def _rms(x, w):
    return x * lax.rsqrt(jnp.mean(x * x, axis=-1, keepdims=True) + NORM_EPS) * w


def _silu(x):
    return x * jax.nn.sigmoid(x)


def _log1p_exp(x):
    return jnp.log(1.0 + jnp.exp(x))


def _dot(a, b):
    return jnp.dot(a.astype(BF16), b.astype(BF16), preferred_element_type=F32)


def _dot_nt(a, b):
    return lax.dot_general(a.astype(BF16), b.astype(BF16), (((1,), (1,)), ((), ())),
                           preferred_element_type=F32)


def _dot_tn(a, b):
    return lax.dot_general(a.astype(BF16), b.astype(BF16), (((0,), (0,)), ((), ())),
                           preferred_element_type=F32)


def _dot_f32(a, b):
    return jnp.dot(a, b, preferred_element_type=F32, precision=lax.Precision.HIGHEST)


def _pair_level(tt, uu):
    x = tt ^ uu
    hb = jnp.zeros(x.shape, jnp.int32)
    for k in range(1, N_LEVELS):
        hb = hb + (x >= (1 << k)).astype(jnp.int32)
    return jnp.where(tt > uu, hb, -1)


def _split3(x):
    hi = x.astype(BF16)
    r = x - hi.astype(F32)
    mid = r.astype(BF16)
    lo = (r - mid.astype(F32)).astype(BF16)
    return jnp.concatenate([hi, mid, lo], axis=0)


def _ffn_kernel(x_ref, nw_ref, wg_ref, wu_ref, wd_ref, fw_ref, o_ref, h_ref, *, final_norm, sub):
    f = pl.program_id(1)
    tm = x_ref.shape[0]

    @pl.when(f == 0)
    def _():
        x = x_ref[...]
        h_ref[...] = _rms(x, nw_ref[...]).astype(BF16)
        o_ref[...] = x

    for r in range(tm // sub):
        rows = slice(r * sub, (r + 1) * sub)
        h = h_ref[rows, :]
        g = jnp.dot(h, wg_ref[...], preferred_element_type=F32)
        u = jnp.dot(h, wu_ref[...], preferred_element_type=F32)
        a = (_silu(g) * (0.5 * u)).astype(BF16)
        o_ref[rows, :] += jnp.dot(a, wd_ref[...], preferred_element_type=F32)

    if final_norm:
        @pl.when(f == pl.num_programs(1) - 1)
        def _():
            o_ref[...] = _rms(o_ref[...], fw_ref[...])


def _ffn(x, nw, wg, wu, wd, fw, *, layer, final_norm, tm, tf, sub):
    n, d = x.shape
    dff = wg.shape[2]
    tm = min(tm, n)
    sub = min(sub, tm)
    return pl.pallas_call(
        functools.partial(_ffn_kernel, final_norm=final_norm, sub=sub),
        out_shape=jax.ShapeDtypeStruct((n, d), F32),
        grid=(n // tm, dff // tf),
        in_specs=[
            pl.BlockSpec((tm, d), lambda i, f: (i, 0), pipeline_mode=pl.Buffered(1)),
            pl.BlockSpec((1, d), lambda i, f: (0, 0)),
            pl.BlockSpec((None, d, tf), lambda i, f: (layer, 0, f)),
            pl.BlockSpec((None, d, tf), lambda i, f: (layer, 0, f)),
            pl.BlockSpec((None, tf, d), lambda i, f: (layer, f, 0)),
            pl.BlockSpec((1, d), lambda i, f: (0, 0)),
        ],
        out_specs=pl.BlockSpec((tm, d), lambda i, f: (i, 0)),
        scratch_shapes=[pltpu.VMEM((tm, d), BF16)],
        compiler_params=_params(("parallel", "arbitrary")),
        name="ffn",
    )(x, nw, wg, wu, wd, fw)


def _inproj_kernel(x_ref, nw_ref, w_ref, o_ref, h_ref):
    @pl.when(pl.program_id(1) == 0)
    def _():
        h_ref[...] = _rms(x_ref[...], nw_ref[...]).astype(BF16)

    o_ref[...] = jnp.dot(h_ref[...], w_ref[...], preferred_element_type=F32)


def _inproj(x, nw, w, *, layer, tm, tn):
    n, d = x.shape
    e = w.shape[2]
    tm = min(tm, n)
    return pl.pallas_call(
        _inproj_kernel,
        out_shape=jax.ShapeDtypeStruct((n, e), F32),
        grid=(n // tm, e // tn),
        in_specs=[
            pl.BlockSpec((tm, d), lambda i, j: (i, 0)),
            pl.BlockSpec((1, d), lambda i, j: (0, 0)),
            pl.BlockSpec((None, d, tn), lambda i, j: (layer, 0, j)),
        ],
        out_specs=pl.BlockSpec((tm, tn), lambda i, j: (i, j)),
        scratch_shapes=[pltpu.VMEM((tm, d), BF16)],
        compiler_params=_params(("parallel", "arbitrary")),
        name="inproj",
    )(x, nw, w)


def _outproj_kernel(x_ref, ya_ref, yb_ref, yc_ref, w_ref, o_ref):
    na, nb = ya_ref.shape[1], yb_ref.shape[1]
    acc = jnp.dot(ya_ref[...], w_ref[0:na, :], preferred_element_type=F32)
    acc += jnp.dot(yb_ref[...], w_ref[na:na + nb, :], preferred_element_type=F32)
    acc += jnp.dot(yc_ref[...], w_ref[na + nb:, :], preferred_element_type=F32)
    o_ref[...] = x_ref[...] + acc


def _outproj(x, ya, yb, yc, w, *, layer, tm):
    n, d = x.shape
    tm = min(tm, n)
    row = lambda a: pl.BlockSpec((tm, a.shape[1]), lambda i: (i, 0))
    return pl.pallas_call(
        _outproj_kernel,
        out_shape=jax.ShapeDtypeStruct((n, d), F32),
        grid=(n // tm,),
        in_specs=[row(x), row(ya), row(yb), row(yc),
                  pl.BlockSpec((None,) + w.shape[1:], lambda i: (layer, 0, 0))],
        out_specs=row(x),
        compiler_params=_params(("parallel",)),
        name="outproj",
    )(x, ya, yb, yc, w)


def _lb_kernel(x_ref, o_ref):
    depth = x_ref.shape[0]
    x = x_ref[...]
    e = jnp.exp(x - jnp.max(x, axis=0, keepdims=True))
    sm = e / jnp.sum(e, axis=0, keepdims=True)
    run = sm[0:1]
    first = run
    for l in range(depth):
        if l > 0:
            run = run + sm[l:l + 1]
        lb = run - first
        o_ref[l] = jnp.concatenate([jnp.log(lb), jnp.log1p(-lb), 1.0 - lb], axis=0)


def _lb_params(lb_logits):
    depth, w = lb_logits.shape
    return pl.pallas_call(
        _lb_kernel,
        out_shape=jax.ShapeDtypeStruct((depth, 3, w), F32),
        name="hgrn_lower_bounds",
    )(lb_logits)


def _hgrn_decay_matrix():
    c = CHUNK
    t = np.arange(c)[:, None]
    u = np.arange(c)[None, :]
    blocks = [(u <= t)]
    for k in range(N_LEVELS - 1, 0, -1):
        s = 1 << k
        ref = (t // (2 * s)) * (2 * s) + s - 1
        odd = (t & s) != 0
        blocks.append(np.where(odd, (u > ref) & (u <= t), (u > t) & (u <= ref)))
    w = np.concatenate(blocks, axis=0).astype(np.float32)
    return np.concatenate([w, w], axis=1)


def _hgrn_kernel(q_ref, f_ref, i_ref, g_ref, lb_ref, nw_ref, w_ref, o_ref,
                 st_ref, oraw_ref, qc_ref, r_ref, el_ref):
    c = CHUNK
    tb = q_ref.shape[0]

    @pl.when(pl.program_id(2) == 0)
    def _():
        st_ref[...] = jnp.zeros_like(st_ref)

    log_lb = lb_ref[0:1, :]
    log_1m = lb_ref[1:2, :]
    one_m = lb_ref[2:3, :]
    nw = nw_ref[...]

    row = lax.broadcasted_iota(jnp.int32, (c, 1), 0)
    tt = lax.broadcasted_iota(jnp.int32, (c, c), 0)
    uu = lax.broadcasted_iota(jnp.int32, (c, c), 1)
    hb = _pair_level(tt, uu)
    odd_of = lambda k: (row & (1 << k)) != 0

    n_chunks = tb // c
    assert n_chunks % 2 == 0
    rows_of = lambda ci: slice(ci * c, (ci + 1) * c)
    qhs, kks, lfs, his, los = [], [], [], [], []
    for ci in range(n_chunks):
        rows = rows_of(ci)
        z = f_ref[rows, :]
        y = log_1m + (jnp.minimum(z, 0.0) - _log1p_exp(-jnp.abs(z)))
        log_f = jnp.maximum(log_lb, y) + _log1p_exp(-jnp.abs(log_lb - y))
        hi = log_f.astype(BF16)
        qhs.append(_silu(q_ref[rows, :]))
        kks.append(one_m * jax.nn.sigmoid(-z))
        lfs.append(log_f)
        his.append(hi)
        los.append((log_f - hi.astype(F32)).astype(BF16))

    dalls = []
    for a in range(0, n_chunks, 2):
        rhs = jnp.concatenate([jnp.concatenate([his[a], his[a + 1]], axis=1),
                               jnp.concatenate([los[a], los[a + 1]], axis=1)], axis=0)
        d = jnp.dot(w_ref[...], rhs, preferred_element_type=F32)
        dalls += [d[:, 0:HEAD_DIM], d[:, HEAD_DIM:2 * HEAD_DIM]]

    ps = [jnp.zeros((c, c), F32) for _ in range(n_chunks)]
    for k in range(N_LEVELS - 1, -1, -1):
        odd = odd_of(k)
        for ci in range(n_chunks):
            li = N_LEVELS - 1 - k
            expo = dalls[ci][(1 + li) * c:(2 + li) * c] if k > 0 else jnp.where(odd, lfs[ci], 0.0)
            xs = (jnp.where(odd, qhs[ci], kks[ci]) * jnp.exp(expo)).astype(BF16)
            pk = lax.dot_general(xs, xs, (((1,), (1,)), ((), ())), preferred_element_type=F32)
            ps[ci] = jnp.where(hb == k, pk, ps[ci])

    for ci in range(n_chunks):
        rows = rows_of(ci)
        v = i_ref[rows, :]
        cum = dalls[ci][0:c]
        last = cum[c - 1:c, :]
        diag = jnp.sum(qhs[ci] * kks[ci], axis=-1, keepdims=True)
        oraw_ref[rows, :] = _dot(ps[ci], v) + diag * v
        qc_ref[rows, :] = (qhs[ci] * jnp.exp(cum)).astype(BF16)
        r_ref[rows, :] = _dot_tn(v, kks[ci] * jnp.exp(last - cum))
        el_ref[ci * SUBLANES:(ci + 1) * SUBLANES, :] = jnp.broadcast_to(jnp.exp(last), (SUBLANES, LANES))

    st = st_ref[...]
    for ci in range(n_chunks):
        rows = rows_of(ci)
        oraw_ref[rows, :] += _dot_nt(qc_ref[rows, :], st)
        st = st * el_ref[ci * SUBLANES:ci * SUBLANES + 1, :] + r_ref[rows, :]
    st_ref[...] = st

    o_ref[...] = (_rms(oraw_ref[...], nw) * _silu(g_ref[...])).astype(o_ref.dtype)


def _hgrn(p, lbp, nw, wmat, *, heads, tb):
    b, t, _ = p.shape
    tb = min(tb, t)
    col = lambda off: pl.BlockSpec((None, tb, LANES), lambda bi, h, ti: (bi, ti, off + h))
    return pl.pallas_call(
        _hgrn_kernel,
        out_shape=jax.ShapeDtypeStruct((b, t, heads * HEAD_DIM), BF16),
        grid=(b, heads, t // tb),
        in_specs=[
            col(0), col(heads), col(2 * heads), col(3 * heads),
            pl.BlockSpec((3, LANES), lambda bi, h, ti: (0, h)),
            pl.BlockSpec((1, LANES), lambda bi, h, ti: (0, 0)),
            pl.BlockSpec(wmat.shape, lambda bi, h, ti: (0, 0)),
        ],
        out_specs=pl.BlockSpec((None, tb, LANES), lambda bi, h, ti: (bi, ti, h)),
        scratch_shapes=[
            pltpu.VMEM((HEAD_DIM, HEAD_DIM), F32),
            pltpu.VMEM((tb, HEAD_DIM), F32),
            pltpu.VMEM((tb, HEAD_DIM), BF16),
            pltpu.VMEM((tb, HEAD_DIM), F32),
            pltpu.VMEM((tb // CHUNK * SUBLANES, LANES), F32),
        ],
        compiler_params=_params(("parallel", "parallel", "arbitrary")),
        name="hgrn2",
    )(p, p, p, p, lbp, nw, wmat)


def _gdn_gates_kernel(gm_ref, ap_ref, tri_ref, o_ref, *, heads):
    c = CHUNK
    gm = gm_ref[...]
    xg = gm + ap_ref[1:2, :]
    g_all = -jnp.exp(ap_ref[0:1, :]) * (jnp.maximum(xg, 0.0) + _log1p_exp(-jnp.abs(xg)))
    lane = lax.broadcasted_iota(jnp.int32, (1, LANES), 1)
    beta = jax.nn.sigmoid(gm)
    for ci in range(gm.shape[0] // c):
        rows = slice(ci * c, (ci + 1) * c)
        gcum = jnp.dot(tri_ref[...], _split3(g_all[rows]), preferred_element_type=F32)
        o_ref[rows, :] = jnp.where(lane < heads, beta[rows], gcum)


def _gdn_gates(p, ap, tri, *, heads, gm_col, tb):
    b, t, _ = p.shape
    tb = min(tb, t)
    return pl.pallas_call(
        functools.partial(_gdn_gates_kernel, heads=heads),
        out_shape=jax.ShapeDtypeStruct((b, t, LANES), F32),
        grid=(b, t // tb),
        in_specs=[
            pl.BlockSpec((None, tb, LANES), lambda bi, ti: (bi, ti, gm_col)),
            pl.BlockSpec(ap.shape, lambda bi, ti: (0, 0)),
            pl.BlockSpec(tri.shape, lambda bi, ti: (0, 0)),
        ],
        out_specs=pl.BlockSpec((None, tb, LANES), lambda bi, ti: (bi, ti, 0)),
        compiler_params=_params(("parallel", "parallel")),
        name="gdn_gates",
    )(p, ap, tri)


def _gdn_kernel(q_ref, k_ref, v_ref, g_ref, gm_ref, cq_ref, ck_ref, cv_ref, nw_ref,
                o_ref, s_ref, halo_ref, ext_ref, qn_ref, kn_ref, vn_ref,
                qd_ref, kd_ref, at_ref, gl_ref, r_ref, pq_ref, oraw_ref, *, head_off, hp):
    c = CHUNK
    tb = q_ref.shape[0]
    h0 = pl.program_id(1) * hp
    pad = SUBLANES

    @pl.when(pl.program_id(2) == 0)
    def _():
        s_ref[...] = jnp.zeros_like(s_ref)
        halo_ref[...] = jnp.zeros_like(halo_ref)

    def conv(idx, src_ref, cw_ref):
        ext_ref[0:pad, :] = halo_ref[idx]
        ext_ref[pad:pad + tb, :] = src_ref[...]
        halo_ref[idx] = src_ref[tb - pad:tb, :]
        acc = cw_ref[CONV_WIDTH - 1:CONV_WIDTH, :] * ext_ref[pad:pad + tb, :]
        for j in range(CONV_WIDTH - 1):
            sh = CONV_WIDTH - 1 - j
            acc = acc + cw_ref[j:j + 1, :] * ext_ref[pad - sh:pad - sh + tb, :]
        return _silu(acc)

    def l2n(a):
        return a * lax.rsqrt(jnp.sum(a * a, axis=-1, keepdims=True) + L2_EPS)

    qc, kc = conv(0, q_ref, cq_ref), conv(1, k_ref, ck_ref)
    vn_ref[...] = conv(2, v_ref, cv_ref)
    for hh in range(hp):
        ln = slice(hh * HEAD_DIM, (hh + 1) * HEAD_DIM)
        qn_ref[:, ln] = l2n(qc[:, ln]) * (HEAD_DIM ** -0.5)
        kn_ref[:, ln] = l2n(kc[:, ln])

    nw = nw_ref[...]
    lane = lax.broadcasted_iota(jnp.int32, (1, LANES), 1)
    tt = lax.broadcasted_iota(jnp.int32, (c, c), 0)
    uu = lax.broadcasted_iota(jnp.int32, (c, c), 1)
    hb = _pair_level(tt, uu)

    n_chunks = tb // c
    units = [(hh, ci) for ci in range(n_chunks) for hh in range(hp)]
    rows_of = lambda ci: slice(ci * c, (ci + 1) * c)
    lanes_of = lambda hh: slice(hh * HEAD_DIM, (hh + 1) * HEAD_DIM)
    gl_rows = lambda hh, ci: (hh * n_chunks + ci) * SUBLANES
    nbs, rhss = [], []
    for hh, ci in units:
        rows, ln = rows_of(ci), lanes_of(hh)
        q = qn_ref[rows, ln]
        k = kn_ref[rows, ln]
        v = vn_ref[rows, ln]
        gm = gm_ref[rows, :]

        beta = jnp.sum(gm * (lane == h0 + hh).astype(F32), axis=-1, keepdims=True)
        gcum = jnp.sum(gm * (lane == h0 + hh + head_off).astype(F32), axis=-1, keepdims=True)
        g_last = gcum[c - 1:c, :]

        gb = jnp.broadcast_to(gcum, (c, c))
        decay = jnp.exp(jnp.where(tt >= uu, gb - gb.T, -jnp.inf))
        kb = k * beta
        m = _dot_nt(jnp.concatenate([kb, q], axis=0), k)
        nbs.append((-(m[0:c] * decay)).astype(BF16))

        e_g = jnp.exp(gcum)
        rhss.append(jnp.concatenate([v * beta, kb * e_g], axis=1))
        qd_ref[rows, ln] = q * e_g
        kd_ref[rows, ln] = (k * jnp.exp(g_last - gcum)).astype(BF16)
        at_ref[hh, rows, :] = (m[c:2 * c] * decay).astype(BF16)
        gl_ref[gl_rows(hh, ci):gl_rows(hh, ci) + SUBLANES, :] = jnp.broadcast_to(jnp.exp(g_last), (SUBLANES, LANES))

    mis = [jnp.where(hb == 0, nb, 0).astype(F32) for nb in nbs]
    one_b = jnp.ones((c, c), BF16)
    for lv in range(1, N_LEVELS):
        o_lvs = [jnp.where(hb == lv, nb, 0) for nb in nbs]
        mbs = [mi.astype(BF16) for mi in mis]
        ys = [jnp.dot(jnp.where(tt == uu, one_b, mb), o_lv, preferred_element_type=F32)
              for mb, o_lv in zip(mbs, o_lvs)]
        mis = [mi + y + jnp.dot(y.astype(BF16), mb, preferred_element_type=F32)
               for mi, y, mb in zip(mis, ys, mbs)]

    sols = [rhs + _dot(mi, rhs) for mi, rhs in zip(mis, rhss)]
    for (hh, ci), sol in zip(units, sols):
        rows, ln = rows_of(ci), lanes_of(hh)
        sol = sol.astype(BF16)
        kd_uw = _dot_tn(kd_ref[rows, ln], sol)
        at_uw = jnp.dot(at_ref[hh, rows, :], sol, preferred_element_type=F32)
        r_ref[hh, rows, :] = kd_uw[:, 0:HEAD_DIM]
        pq_ref[hh, ci, 0:c, :] = kd_uw[:, HEAD_DIM:2 * HEAD_DIM].astype(BF16)
        pq_ref[hh, ci, c:2 * c, :] = (qd_ref[rows, ln] - at_uw[:, HEAD_DIM:2 * HEAD_DIM]).astype(BF16)
        oraw_ref[rows, ln] = at_uw[:, 0:HEAD_DIM]

    ss = [s_ref[hh] for hh in range(hp)]
    for ci in range(n_chunks):
        rows = rows_of(ci)
        zs = [jnp.dot(pq_ref[hh, ci], ss[hh].astype(BF16), preferred_element_type=F32) for hh in range(hp)]
        for hh in range(hp):
            oraw_ref[rows, lanes_of(hh)] += zs[hh][c:2 * c]
            ss[hh] = (ss[hh] * gl_ref[gl_rows(hh, ci):gl_rows(hh, ci) + 1, :] - zs[hh][0:c]
                      + r_ref[hh, rows, :])
    for hh in range(hp):
        s_ref[hh] = ss[hh]

    g = _silu(g_ref[...])
    for hh in range(hp):
        ln = lanes_of(hh)
        o_ref[:, ln] = (_rms(oraw_ref[:, ln], nw) * g[:, ln]).astype(o_ref.dtype)


def _gdn(p, gates, conv_w, nw, *, heads, col0, tb, hp):
    b, t, _ = p.shape
    tb = min(tb, t)
    w = hp * HEAD_DIM
    n_chunks = tb // CHUNK
    col = lambda off: pl.BlockSpec((None, tb, w), lambda bi, h, ti: (bi, ti, off // hp + h))
    cw = lambda off: pl.BlockSpec((CONV_WIDTH, w), lambda bi, h, ti: (0, off // hp + h))
    return pl.pallas_call(
        functools.partial(_gdn_kernel, head_off=heads, hp=hp),
        out_shape=jax.ShapeDtypeStruct((b, t, heads * HEAD_DIM), BF16),
        grid=(b, heads // hp, t // tb),
        in_specs=[
            col(col0), col(col0 + heads), col(col0 + 2 * heads), col(col0 + 3 * heads),
            pl.BlockSpec((None, tb, LANES), lambda bi, h, ti: (bi, ti, 0)),
            cw(0), cw(heads), cw(2 * heads),
            pl.BlockSpec(nw.shape, lambda bi, h, ti: (0, 0)),
        ],
        out_specs=pl.BlockSpec((None, tb, w), lambda bi, h, ti: (bi, ti, h)),
        scratch_shapes=[
            pltpu.VMEM((hp, HEAD_DIM, HEAD_DIM), F32),
            pltpu.VMEM((3, SUBLANES, w), F32),
            pltpu.VMEM((tb + SUBLANES, w), F32),
            pltpu.VMEM((tb, w), F32),
            pltpu.VMEM((tb, w), F32),
            pltpu.VMEM((tb, w), F32),
            pltpu.VMEM((tb, w), F32),
            pltpu.VMEM((tb, w), BF16),
            pltpu.VMEM((hp, tb, CHUNK), BF16),
            pltpu.VMEM((hp * n_chunks * SUBLANES, LANES), F32),
            pltpu.VMEM((hp, tb, HEAD_DIM), F32),
            pltpu.VMEM((hp, n_chunks, 2 * CHUNK, HEAD_DIM), BF16),
            pltpu.VMEM((tb, w), F32),
        ],
        compiler_params=_params(("parallel", "parallel", "arbitrary")),
        name="gated_deltanet",
    )(p, p, p, p, gates, conv_w, conv_w, conv_w, nw)


def _pool_kernel(u_ref, w_ref, sc_ref, o_ref, ext_ref):
    tb = u_ref.shape[0]
    ti = pl.program_id(1)

    @pl.when(ti == 0)
    def _():
        ext_ref[0:POOL_MAX, :] = jnp.zeros((POOL_MAX, ext_ref.shape[1]), F32)

    ext_ref[POOL_MAX:POOL_MAX + tb, :] = u_ref[...]
    pos = ti * tb + lax.broadcasted_iota(jnp.int32, (tb, 1), 0)
    for gi, win in enumerate(POOL_WINDOWS):
        sl = slice(gi * LANES, (gi + 1) * LANES)
        cur = ext_ref[POOL_MAX:POOL_MAX + tb, sl]
        ws = cur
        for j in range(1, win):
            ws = ws + ext_ref[POOL_MAX - j:POOL_MAX - j + tb, sl]
        count = jnp.minimum(pos + 1, win).astype(F32)
        m = ws / count - cur
        o_ref[:, sl] = (_dot(m, w_ref[gi]) * sc_ref[:, sl]).astype(o_ref.dtype)
    ext_ref[0:POOL_MAX, :] = ext_ref[tb:tb + POOL_MAX, :]


def _pool(p, pool_w, scale, *, colblk, tb):
    b, t, _ = p.shape
    tb = min(tb, t)
    width = len(POOL_WINDOWS) * LANES
    return pl.pallas_call(
        _pool_kernel,
        out_shape=jax.ShapeDtypeStruct((b, t, width), BF16),
        grid=(b, t // tb),
        in_specs=[
            pl.BlockSpec((None, tb, width), lambda bi, ti: (bi, ti, colblk)),
            pl.BlockSpec(pool_w.shape, lambda bi, ti: (0, 0, 0)),
            pl.BlockSpec((1, width), lambda bi, ti: (0, 0)),
        ],
        out_specs=pl.BlockSpec((None, tb, width), lambda bi, ti: (bi, ti, 0)),
        scratch_shapes=[pltpu.VMEM((tb + POOL_MAX, width), F32)],
        compiler_params=_params(("parallel", "arbitrary")),
        name="causal_pool",
    )(p, pool_w, scale)


def kernel(x, lb_logits, norm_ffn1, ffn1_w_gate, ffn1_w_up, ffn1_w_down, norm_mix, w_in, gdn_conv_w, gdn_a_log, gdn_dt_bias, hgrn_norm_w, gdn_norm_w, pool_w, pool_scale, w_out, norm_ffn2, ffn2_w_gate, ffn2_w_up, ffn2_w_down, norm_final):
    bsz, seq, d = x.shape
    depth = lb_logits.shape[0]
    hgrn_w = lb_logits.shape[1]
    gdn_heads = gdn_a_log.shape[1]
    gdn_w = gdn_heads * HEAD_DIM
    pool_width = pool_scale.shape[1]
    hgrn_heads = hgrn_w // HEAD_DIM
    n = bsz * seq

    main = 4 * hgrn_w + 4 * gdn_w
    e_used = main + pool_width + LANES
    tn_in = 768
    e_pad = -(-e_used // tn_in) * tn_in
    w_in_b = w_in.astype(BF16)
    w_in_r = jnp.concatenate(
        [w_in_b[:, :, :main], w_in_b[:, :, main + 2 * gdn_heads:], w_in_b[:, :, main:main + 2 * gdn_heads],
         jnp.zeros((depth, d, e_pad - e_used + LANES - 2 * gdn_heads), BF16)], axis=-1)
    gdn_col0 = 4 * hgrn_heads
    pool_colblk = main // pool_width
    gm_col = (main + pool_width) // LANES

    lane_pad = lambda a: jnp.pad(a, ((0, 0), (gdn_heads, LANES - 2 * gdn_heads)))
    gdn_ap = jnp.stack([lane_pad(gdn_a_log), lane_pad(gdn_dt_bias)], axis=1)

    hgrn_wmat = jnp.asarray(_hgrn_decay_matrix(), BF16)
    tri = np.tril(np.ones((CHUNK, CHUNK), np.float32))
    tri3 = jnp.asarray(np.concatenate([tri, tri, tri], axis=1), BF16)

    lbp = _lb_params(lb_logits.astype(F32))

    w_out_b = w_out.astype(BF16)

    ffn1_w = [w.astype(BF16) for w in (ffn1_w_gate, ffn1_w_up, ffn1_w_down)]
    ffn2_w = [w.astype(BF16) for w in (ffn2_w_gate, ffn2_w_up, ffn2_w_down)]
    pool_w_b = pool_w.astype(BF16)

    xf = x.reshape(n, d)
    row = lambda a: a.reshape(1, -1)
    ffn_tiles = dict(tm=1024, tf=512, sub=512)
    for l in range(depth):
        xf = _ffn(xf, row(norm_ffn1[l]), *ffn1_w, row(norm_final), layer=l, final_norm=False, **ffn_tiles)
        p = _inproj(xf, row(norm_mix[l]), w_in_r, layer=l, tm=1024, tn=tn_in).reshape(bsz, seq, e_pad)
        ya = _hgrn(p, lbp[l], row(hgrn_norm_w[l]), hgrn_wmat, heads=hgrn_heads, tb=1024)
        gates = _gdn_gates(p, gdn_ap[l], tri3, heads=gdn_heads, gm_col=gm_col, tb=1024)
        yb = _gdn(p, gates, gdn_conv_w[l], row(gdn_norm_w[l]), heads=gdn_heads, col0=gdn_col0, tb=512, hp=4)
        yc = _pool(p, pool_w_b[l], row(pool_scale[l]), colblk=pool_colblk, tb=512)
        xf = _outproj(xf, ya.reshape(n, -1), yb.reshape(n, -1), yc.reshape(n, -1), w_out_b, layer=l, tm=512)
        xf = _ffn(xf, row(norm_ffn2[l]), *ffn2_w, row(norm_final), layer=l,
                  final_norm=(l == depth - 1), **ffn_tiles)
    return xf.reshape(bsz, seq, d)
```

```python
import functools

import numpy as np
import jax
import jax.numpy as jnp
from jax import lax
from jax.experimental import pallas as pl
from jax.experimental.pallas import tpu as pltpu

HEAD_DIM = 128
POOL_WINDOWS = (2, 4, 8, 16)
POOL_MAX = max(POOL_WINDOWS)
CONV_WIDTH = 4
NORM_EPS = 1e-6
L2_EPS = 1e-6

LANES = 128
SUBLANES = 8
VMEM_LIMIT_BYTES = 56 * 1024 * 1024

CHUNK = 128
N_LEVELS = 7

F32 = jnp.float32
BF16 = jnp.bfloat16


def _params(sem):
    return pltpu.CompilerParams(dimension_semantics=sem, vmem_limit_bytes=VMEM_LIMIT_BYTES)


def _rms(x, w):
    return x * lax.rsqrt(jnp.mean(x * x, axis=-1, keepdims=True) + NORM_EPS) * w


def _silu(x):
    return x * jax.nn.sigmoid(x)


def _log1p_exp(x):
    return jnp.log(1.0 + jnp.exp(x))


def _dot(a, b):
    return jnp.dot(a.astype(BF16), b.astype(BF16), preferred_element_type=F32)


def _dot_nt(a, b):
    return lax.dot_general(a.astype(BF16), b.astype(BF16), (((1,), (1,)), ((), ())),
                           preferred_element_type=F32)


def _dot_tn(a, b):
    return lax.dot_general(a.astype(BF16), b.astype(BF16), (((0,), (0,)), ((), ())),
                           preferred_element_type=F32)


def _dot_f32(a, b):
    return jnp.dot(a, b, preferred_element_type=F32, precision=lax.Precision.HIGHEST)


def _pair_level(tt, uu):
    x = tt ^ uu
    hb = jnp.zeros(x.shape, jnp.int32)
    for k in range(1, N_LEVELS):
        hb = hb + (x >= (1 << k)).astype(jnp.int32)
    return jnp.where(tt > uu, hb, -1)


def _split3(x):
    hi = x.astype(BF16)
    r = x - hi.astype(F32)
    mid = r.astype(BF16)
    lo = (r - mid.astype(F32)).astype(BF16)
    return jnp.concatenate([hi, mid, lo], axis=0)


def _ffn_kernel(x_ref, nw_ref, wg_ref, wu_ref, wd_ref, fw_ref, o_ref, h_ref, *, final_norm, sub):
    f = pl.program_id(1)
    tm = x_ref.shape[0]

    @pl.when(f == 0)
    def _():
        x = x_ref[...]
        h_ref[...] = _rms(x, nw_ref[...]).astype(BF16)
        o_ref[...] = x

    for r in range(tm // sub):
        rows = slice(r * sub, (r + 1) * sub)
        h = h_ref[rows, :]
        g = jnp.dot(h, wg_ref[...], preferred_element_type=F32)
        u = jnp.dot(h, wu_ref[...], preferred_element_type=F32)
        a = (_silu(g) * (0.5 * u)).astype(BF16)
        o_ref[rows, :] += jnp.dot(a, wd_ref[...], preferred_element_type=F32)

    if final_norm:
        @pl.when(f == pl.num_programs(1) - 1)
        def _():
            o_ref[...] = _rms(o_ref[...], fw_ref[...])


def _ffn(x, nw, wg, wu, wd, fw, *, layer, final_norm, tm, tf, sub):
    n, d = x.shape
    dff = wg.shape[2]
    tm = min(tm, n)
    sub = min(sub, tm)
    return pl.pallas_call(
        functools.partial(_ffn_kernel, final_norm=final_norm, sub=sub),
        out_shape=jax.ShapeDtypeStruct((n, d), F32),
        grid=(n // tm, dff // tf),
        in_specs=[
            pl.BlockSpec((tm, d), lambda i, f: (i, 0)),
            pl.BlockSpec((1, d), lambda i, f: (0, 0)),
            pl.BlockSpec((None, d, tf), lambda i, f: (layer, 0, f)),
            pl.BlockSpec((None, d, tf), lambda i, f: (layer, 0, f)),
            pl.BlockSpec((None, tf, d), lambda i, f: (layer, f, 0)),
            pl.BlockSpec((1, d), lambda i, f: (0, 0)),
        ],
        out_specs=pl.BlockSpec((tm, d), lambda i, f: (i, 0)),
        scratch_shapes=[pltpu.VMEM((tm, d), BF16)],
        compiler_params=_params(("parallel", "arbitrary")),
        name="ffn",
    )(x, nw, wg, wu, wd, fw)


def _inproj_kernel(x_ref, nw_ref, w_ref, o_ref, h_ref):
    @pl.when(pl.program_id(1) == 0)
    def _():
        h_ref[...] = _rms(x_ref[...], nw_ref[...]).astype(BF16)

    o_ref[...] = jnp.dot(h_ref[...], w_ref[...], preferred_element_type=F32)


def _inproj(x, nw, w, *, layer, tm, tn):
    n, d = x.shape
    e = w.shape[2]
    tm = min(tm, n)
    return pl.pallas_call(
        _inproj_kernel,
        out_shape=jax.ShapeDtypeStruct((n, e), F32),
        grid=(n // tm, e // tn),
        in_specs=[
            pl.BlockSpec((tm, d), lambda i, j: (i, 0)),
            pl.BlockSpec((1, d), lambda i, j: (0, 0)),
            pl.BlockSpec((None, d, tn), lambda i, j: (layer, 0, j)),
        ],
        out_specs=pl.BlockSpec((tm, tn), lambda i, j: (i, j)),
        scratch_shapes=[pltpu.VMEM((tm, d), BF16)],
        compiler_params=_params(("parallel", "arbitrary")),
        name="inproj",
    )(x, nw, w)


def _w_in_prep_kernel(w_ref, o_ref, *, main, nscal):
    e = w_ref.shape[1]
    e_pad = o_ref.shape[1]
    rest = e - main - nscal
    o_ref[:, 0:main] = w_ref[:, 0:main].astype(BF16)
    o_ref[:, main:main + rest] = w_ref[:, main + nscal:e].astype(BF16)
    tail = jnp.concatenate([w_ref[:, main:main + nscal].astype(BF16),
                            jnp.zeros((w_ref.shape[0], e_pad - e), BF16)], axis=1)
    o_ref[:, main + rest:e_pad] = tail


def _w_in_prep(w_in, *, main, nscal, e_pad, tr):
    depth, d, e = w_in.shape
    return pl.pallas_call(
        functools.partial(_w_in_prep_kernel, main=main, nscal=nscal),
        out_shape=jax.ShapeDtypeStruct((depth, d, e_pad), BF16),
        grid=(depth, d // tr),
        in_specs=[pl.BlockSpec((None, tr, e), lambda l, i: (l, i, 0))],
        out_specs=pl.BlockSpec((None, tr, e_pad), lambda l, i: (l, i, 0)),
        compiler_params=_params(("parallel", "parallel")),
        name="w_in_regroup",
    )(w_in)


def _outproj_kernel(x_ref, ya_ref, yb_ref, yc_ref, w_ref, o_ref):
    na, nb = ya_ref.shape[1], yb_ref.shape[1]
    acc = jnp.dot(ya_ref[...], w_ref[0:na, :], preferred_element_type=F32)
    acc += jnp.dot(yb_ref[...], w_ref[na:na + nb, :], preferred_element_type=F32)
    acc += jnp.dot(yc_ref[...], w_ref[na + nb:, :], preferred_element_type=F32)
    o_ref[...] = x_ref[...] + acc


def _outproj(x, ya, yb, yc, w, *, layer, tm):
    n, d = x.shape
    tm = min(tm, n)
    row = lambda a: pl.BlockSpec((tm, a.shape[1]), lambda i: (i, 0))
    return pl.pallas_call(
        _outproj_kernel,
        out_shape=jax.ShapeDtypeStruct((n, d), F32),
        grid=(n // tm,),
        in_specs=[row(x), row(ya), row(yb), row(yc),
                  pl.BlockSpec((None,) + w.shape[1:], lambda i: (layer, 0, 0))],
        out_specs=row(x),
        compiler_params=_params(("parallel",)),
        name="outproj",
    )(x, ya, yb, yc, w)


def _lb_kernel(x_ref, o_ref):
    depth = x_ref.shape[0]
    x = x_ref[...]
    e = jnp.exp(x - jnp.max(x, axis=0, keepdims=True))
    sm = e / jnp.sum(e, axis=0, keepdims=True)
    run = sm[0:1]
    first = run
    for l in range(depth):
        if l > 0:
            run = run + sm[l:l + 1]
        lb = run - first
        o_ref[l] = jnp.concatenate([jnp.log(lb), jnp.log1p(-lb), 1.0 - lb], axis=0)


def _lb_params(lb_logits):
    depth, w = lb_logits.shape
    return pl.pallas_call(
        _lb_kernel,
        out_shape=jax.ShapeDtypeStruct((depth, 3, w), F32),
        name="hgrn_lower_bounds",
    )(lb_logits)


def _hgrn_decay_matrix():
    c = CHUNK
    t = np.arange(c)[:, None]
    u = np.arange(c)[None, :]
    blocks = [(u <= t)]
    for k in range(N_LEVELS - 1, 0, -1):
        s = 1 << k
        ref = (t // (2 * s)) * (2 * s) + s - 1
        odd = (t & s) != 0
        blocks.append(np.where(odd, (u > ref) & (u <= t), (u > t) & (u <= ref)))
    w = np.concatenate(blocks, axis=0).astype(np.float32)
    return np.concatenate([w, w], axis=1)


def _hgrn_kernel(q_ref, f_ref, i_ref, g_ref, lb_ref, nw_ref, w_ref, o_ref,
                 st_ref, oraw_ref, qc_ref, r_ref, el_ref):
    c = CHUNK
    tb = q_ref.shape[0]

    @pl.when(pl.program_id(2) == 0)
    def _():
        st_ref[...] = jnp.zeros_like(st_ref)

    log_lb = lb_ref[0:1, :]
    log_1m = lb_ref[1:2, :]
    one_m = lb_ref[2:3, :]
    nw = nw_ref[...]

    row = lax.broadcasted_iota(jnp.int32, (c, 1), 0)
    tt = lax.broadcasted_iota(jnp.int32, (c, c), 0)
    uu = lax.broadcasted_iota(jnp.int32, (c, c), 1)
    hb = _pair_level(tt, uu)
    odd_of = lambda k: (row & (1 << k)) != 0

    n_chunks = tb // c
    assert n_chunks % 2 == 0
    rows_of = lambda ci: slice(ci * c, (ci + 1) * c)
    qhs, kks, lfs, his, los = [], [], [], [], []
    for ci in range(n_chunks):
        rows = rows_of(ci)
        z = f_ref[rows, :]
        y = log_1m + (jnp.minimum(z, 0.0) - _log1p_exp(-jnp.abs(z)))
        log_f = jnp.maximum(log_lb, y) + _log1p_exp(-jnp.abs(log_lb - y))
        hi = log_f.astype(BF16)
        qhs.append(_silu(q_ref[rows, :]))
        kks.append(one_m * jax.nn.sigmoid(-z))
        lfs.append(log_f)
        his.append(hi)
        los.append((log_f - hi.astype(F32)).astype(BF16))

    dalls = []
    for a in range(0, n_chunks, 2):
        rhs = jnp.concatenate([jnp.concatenate([his[a], his[a + 1]], axis=1),
                               jnp.concatenate([los[a], los[a + 1]], axis=1)], axis=0)
        d = jnp.dot(w_ref[...], rhs, preferred_element_type=F32)
        dalls += [d[:, 0:HEAD_DIM], d[:, HEAD_DIM:2 * HEAD_DIM]]

    ps = [jnp.zeros((c, c), F32) for _ in range(n_chunks)]
    for k in range(N_LEVELS - 1, -1, -1):
        odd = odd_of(k)
        for ci in range(n_chunks):
            li = N_LEVELS - 1 - k
            expo = dalls[ci][(1 + li) * c:(2 + li) * c] if k > 0 else jnp.where(odd, lfs[ci], 0.0)
            xs = (jnp.where(odd, qhs[ci], kks[ci]) * jnp.exp(expo)).astype(BF16)
            pk = lax.dot_general(xs, xs, (((1,), (1,)), ((), ())), preferred_element_type=F32)
            ps[ci] = jnp.where(hb == k, pk, ps[ci])

    for ci in range(n_chunks):
        rows = rows_of(ci)
        v = i_ref[rows, :]
        cum = dalls[ci][0:c]
        last = cum[c - 1:c, :]
        diag = jnp.sum(qhs[ci] * kks[ci], axis=-1, keepdims=True)
        oraw_ref[rows, :] = _dot(ps[ci], v) + diag * v
        qc_ref[rows, :] = (qhs[ci] * jnp.exp(cum)).astype(BF16)
        r_ref[rows, :] = _dot_tn(v, kks[ci] * jnp.exp(last - cum))
        el_ref[ci * SUBLANES:(ci + 1) * SUBLANES, :] = jnp.broadcast_to(jnp.exp(last), (SUBLANES, LANES))

    st = st_ref[...]
    for ci in range(n_chunks):
        rows = rows_of(ci)
        oraw_ref[rows, :] += _dot_nt(qc_ref[rows, :], st)
        st = st * el_ref[ci * SUBLANES:ci * SUBLANES + 1, :] + r_ref[rows, :]
    st_ref[...] = st

    o_ref[...] = (_rms(oraw_ref[...], nw) * _silu(g_ref[...])).astype(o_ref.dtype)


def _hgrn(p, lbp, nw, wmat, *, heads, tb):
    b, t, _ = p.shape
    tb = min(tb, t)
    col = lambda off: pl.BlockSpec((None, tb, LANES), lambda bi, h, ti: (bi, ti, off + h))
    return pl.pallas_call(
        _hgrn_kernel,
        out_shape=jax.ShapeDtypeStruct((b, t, heads * HEAD_DIM), BF16),
        grid=(b, heads, t // tb),
        in_specs=[
            col(0), col(heads), col(2 * heads), col(3 * heads),
            pl.BlockSpec((3, LANES), lambda bi, h, ti: (0, h)),
            pl.BlockSpec((1, LANES), lambda bi, h, ti: (0, 0)),
            pl.BlockSpec(wmat.shape, lambda bi, h, ti: (0, 0)),
        ],
        out_specs=pl.BlockSpec((None, tb, LANES), lambda bi, h, ti: (bi, ti, h)),
        scratch_shapes=[
            pltpu.VMEM((HEAD_DIM, HEAD_DIM), F32),
            pltpu.VMEM((tb, HEAD_DIM), F32),
            pltpu.VMEM((tb, HEAD_DIM), BF16),
            pltpu.VMEM((tb, HEAD_DIM), F32),
            pltpu.VMEM((tb // CHUNK * SUBLANES, LANES), F32),
        ],
        compiler_params=_params(("parallel", "parallel", "arbitrary")),
        name="hgrn2",
    )(p, p, p, p, lbp, nw, wmat)


def _gdn_gates_kernel(gm_ref, ap_ref, tri_ref, o_ref, *, heads):
    c = CHUNK
    gm = gm_ref[...]
    xg = gm + ap_ref[1:2, :]
    g_all = -jnp.exp(ap_ref[0:1, :]) * (jnp.maximum(xg, 0.0) + _log1p_exp(-jnp.abs(xg)))
    lane = lax.broadcasted_iota(jnp.int32, (1, LANES), 1)
    beta = jax.nn.sigmoid(gm)
    for ci in range(gm.shape[0] // c):
        rows = slice(ci * c, (ci + 1) * c)
        gcum = jnp.dot(tri_ref[...], _split3(g_all[rows]), preferred_element_type=F32)
        o_ref[rows, :] = jnp.where(lane < heads, beta[rows], gcum)


def _gdn_gates(p, ap, tri, *, heads, gm_col, tb):
    b, t, _ = p.shape
    tb = min(tb, t)
    return pl.pallas_call(
        functools.partial(_gdn_gates_kernel, heads=heads),
        out_shape=jax.ShapeDtypeStruct((b, t, LANES), F32),
        grid=(b, t // tb),
        in_specs=[
            pl.BlockSpec((None, tb, LANES), lambda bi, ti: (bi, ti, gm_col)),
            pl.BlockSpec(ap.shape, lambda bi, ti: (0, 0)),
            pl.BlockSpec(tri.shape, lambda bi, ti: (0, 0)),
        ],
        out_specs=pl.BlockSpec((None, tb, LANES), lambda bi, ti: (bi, ti, 0)),
        compiler_params=_params(("parallel", "parallel")),
        name="gdn_gates",
    )(p, ap, tri)


def _gdn_kernel(q_ref, k_ref, v_ref, g_ref, gm_ref, cq_ref, ck_ref, cv_ref, nw_ref,
                o_ref, s_ref, halo_ref, ext_ref, qn_ref, kn_ref, vn_ref,
                qd_ref, kd_ref, at_ref, gl_ref, r_ref, pq_ref, oraw_ref, *, head_off, hp):
    c = CHUNK
    tb = q_ref.shape[0]
    h0 = pl.program_id(1) * hp
    pad = SUBLANES

    @pl.when(pl.program_id(2) == 0)
    def _():
        s_ref[...] = jnp.zeros_like(s_ref)
        halo_ref[...] = jnp.zeros_like(halo_ref)

    def conv(idx, src_ref, cw_ref):
        ext_ref[0:pad, :] = halo_ref[idx]
        ext_ref[pad:pad + tb, :] = src_ref[...]
        halo_ref[idx] = src_ref[tb - pad:tb, :]
        acc = cw_ref[CONV_WIDTH - 1:CONV_WIDTH, :] * ext_ref[pad:pad + tb, :]
        for j in range(CONV_WIDTH - 1):
            sh = CONV_WIDTH - 1 - j
            acc = acc + cw_ref[j:j + 1, :] * ext_ref[pad - sh:pad - sh + tb, :]
        return _silu(acc)

    def l2n(a):
        return a * lax.rsqrt(jnp.sum(a * a, axis=-1, keepdims=True) + L2_EPS)

    qc, kc = conv(0, q_ref, cq_ref), conv(1, k_ref, ck_ref)
    vn_ref[...] = conv(2, v_ref, cv_ref)
    for hh in range(hp):
        ln = slice(hh * HEAD_DIM, (hh + 1) * HEAD_DIM)
        qn_ref[:, ln] = l2n(qc[:, ln]) * (HEAD_DIM ** -0.5)
        kn_ref[:, ln] = l2n(kc[:, ln])

    nw = nw_ref[...]
    lane = lax.broadcasted_iota(jnp.int32, (1, LANES), 1)
    tt = lax.broadcasted_iota(jnp.int32, (c, c), 0)
    uu = lax.broadcasted_iota(jnp.int32, (c, c), 1)
    hb = _pair_level(tt, uu)

    n_chunks = tb // c
    units = [(hh, ci) for ci in range(n_chunks) for hh in range(hp)]
    rows_of = lambda ci: slice(ci * c, (ci + 1) * c)
    lanes_of = lambda hh: slice(hh * HEAD_DIM, (hh + 1) * HEAD_DIM)
    gl_rows = lambda hh, ci: (hh * n_chunks + ci) * SUBLANES
    nbs, rhss = [], []
    for hh, ci in units:
        rows, ln = rows_of(ci), lanes_of(hh)
        q = qn_ref[rows, ln]
        k = kn_ref[rows, ln]
        v = vn_ref[rows, ln]
        gm = gm_ref[rows, :]

        beta = jnp.sum(gm * (lane == h0 + hh).astype(F32), axis=-1, keepdims=True)
        gcum = jnp.sum(gm * (lane == h0 + hh + head_off).astype(F32), axis=-1, keepdims=True)
        g_last = gcum[c - 1:c, :]

        gb = jnp.broadcast_to(gcum, (c, c))
        decay = jnp.exp(jnp.where(tt >= uu, gb - gb.T, -jnp.inf))
        kb = k * beta
        m = _dot_nt(jnp.concatenate([kb, q], axis=0), k)
        nbs.append((-(m[0:c] * decay)).astype(BF16))

        e_g = jnp.exp(gcum)
        rhss.append(jnp.concatenate([v * beta, kb * e_g], axis=1))
        qd_ref[rows, ln] = q * e_g
        kd_ref[rows, ln] = (k * jnp.exp(g_last - gcum)).astype(BF16)
        at_ref[hh, rows, :] = (m[c:2 * c] * decay).astype(BF16)
        gl_ref[gl_rows(hh, ci):gl_rows(hh, ci) + SUBLANES, :] = jnp.broadcast_to(jnp.exp(g_last), (SUBLANES, LANES))

    mis = [jnp.where(hb == 0, nb, 0).astype(F32) for nb in nbs]
    one_b = jnp.ones((c, c), BF16)
    for lv in range(1, N_LEVELS):
        o_lvs = [jnp.where(hb == lv, nb, 0) for nb in nbs]
        mbs = [mi.astype(BF16) for mi in mis]
        ys = [jnp.dot(jnp.where(tt == uu, one_b, mb), o_lv, preferred_element_type=F32)
              for mb, o_lv in zip(mbs, o_lvs)]
        mis = [mi + y + jnp.dot(y.astype(BF16), mb, preferred_element_type=F32)
               for mi, y, mb in zip(mis, ys, mbs)]

    sols = [rhs + _dot(mi, rhs) for mi, rhs in zip(mis, rhss)]
    for (hh, ci), sol in zip(units, sols):
        rows, ln = rows_of(ci), lanes_of(hh)
        sol = sol.astype(BF16)
        kd_uw = _dot_tn(kd_ref[rows, ln], sol)
        at_uw = jnp.dot(at_ref[hh, rows, :], sol, preferred_element_type=F32)
        r_ref[hh, rows, :] = kd_uw[:, 0:HEAD_DIM]
        pq_ref[hh, ci, 0:c, :] = kd_uw[:, HEAD_DIM:2 * HEAD_DIM].astype(BF16)
        pq_ref[hh, ci, c:2 * c, :] = (qd_ref[rows, ln] - at_uw[:, HEAD_DIM:2 * HEAD_DIM]).astype(BF16)
        oraw_ref[rows, ln] = at_uw[:, 0:HEAD_DIM]

    ss = [s_ref[hh] for hh in range(hp)]
    for ci in range(n_chunks):
        rows = rows_of(ci)
        zs = [jnp.dot(pq_ref[hh, ci], ss[hh].astype(BF16), preferred_element_type=F32) for hh in range(hp)]
        for hh in range(hp):
            oraw_ref[rows, lanes_of(hh)] += zs[hh][c:2 * c]
            ss[hh] = (ss[hh] * gl_ref[gl_rows(hh, ci):gl_rows(hh, ci) + 1, :] - zs[hh][0:c]
                      + r_ref[hh, rows, :])
    for hh in range(hp):
        s_ref[hh] = ss[hh]

    g = _silu(g_ref[...])
    for hh in range(hp):
        ln = lanes_of(hh)
        o_ref[:, ln] = (_rms(oraw_ref[:, ln], nw) * g[:, ln]).astype(o_ref.dtype)


def _gdn(p, gates, conv_w, nw, *, heads, col0, tb, hp):
    b, t, _ = p.shape
    tb = min(tb, t)
    w = hp * HEAD_DIM
    n_chunks = tb // CHUNK
    col = lambda off: pl.BlockSpec((None, tb, w), lambda bi, h, ti: (bi, ti, off // hp + h))
    cw = lambda off: pl.BlockSpec((CONV_WIDTH, w), lambda bi, h, ti: (0, off // hp + h))
    return pl.pallas_call(
        functools.partial(_gdn_kernel, head_off=heads, hp=hp),
        out_shape=jax.ShapeDtypeStruct((b, t, heads * HEAD_DIM), BF16),
        grid=(b, heads // hp, t // tb),
        in_specs=[
            col(col0), col(col0 + heads), col(col0 + 2 * heads), col(col0 + 3 * heads),
            pl.BlockSpec((None, tb, LANES), lambda bi, h, ti: (bi, ti, 0)),
            cw(0), cw(heads), cw(2 * heads),
            pl.BlockSpec(nw.shape, lambda bi, h, ti: (0, 0)),
        ],
        out_specs=pl.BlockSpec((None, tb, w), lambda bi, h, ti: (bi, ti, h)),
        scratch_shapes=[
            pltpu.VMEM((hp, HEAD_DIM, HEAD_DIM), F32),
            pltpu.VMEM((3, SUBLANES, w), F32),
            pltpu.VMEM((tb + SUBLANES, w), F32),
            pltpu.VMEM((tb, w), F32),
            pltpu.VMEM((tb, w), F32),
            pltpu.VMEM((tb, w), F32),
            pltpu.VMEM((tb, w), F32),
            pltpu.VMEM((tb, w), BF16),
            pltpu.VMEM((hp, tb, CHUNK), BF16),
            pltpu.VMEM((hp * n_chunks * SUBLANES, LANES), F32),
            pltpu.VMEM((hp, tb, HEAD_DIM), F32),
            pltpu.VMEM((hp, n_chunks, 2 * CHUNK, HEAD_DIM), BF16),
            pltpu.VMEM((tb, w), F32),
        ],
        compiler_params=_params(("parallel", "parallel", "arbitrary")),
        name="gated_deltanet",
    )(p, p, p, p, gates, conv_w, conv_w, conv_w, nw)


def _pool_kernel(u_ref, w_ref, sc_ref, o_ref, ext_ref):
    tb = u_ref.shape[0]
    ti = pl.program_id(1)

    @pl.when(ti == 0)
    def _():
        ext_ref[0:POOL_MAX, :] = jnp.zeros((POOL_MAX, ext_ref.shape[1]), F32)

    ext_ref[POOL_MAX:POOL_MAX + tb, :] = u_ref[...]
    pos = ti * tb + lax.broadcasted_iota(jnp.int32, (tb, 1), 0)
    for gi, win in enumerate(POOL_WINDOWS):
        sl = slice(gi * LANES, (gi + 1) * LANES)
        cur = ext_ref[POOL_MAX:POOL_MAX + tb, sl]
        ws = cur
        for j in range(1, win):
            ws = ws + ext_ref[POOL_MAX - j:POOL_MAX - j + tb, sl]
        count = jnp.minimum(pos + 1, win).astype(F32)
        m = ws / count - cur
        o_ref[:, sl] = (_dot(m, w_ref[gi]) * sc_ref[:, sl]).astype(o_ref.dtype)
    ext_ref[0:POOL_MAX, :] = ext_ref[tb:tb + POOL_MAX, :]


def _pool(p, pool_w, scale, *, colblk, tb):
    b, t, _ = p.shape
    tb = min(tb, t)
    width = len(POOL_WINDOWS) * LANES
    return pl.pallas_call(
        _pool_kernel,
        out_shape=jax.ShapeDtypeStruct((b, t, width), BF16),
        grid=(b, t // tb),
        in_specs=[
            pl.BlockSpec((None, tb, width), lambda bi, ti: (bi, ti, colblk)),
            pl.BlockSpec(pool_w.shape, lambda bi, ti: (0, 0, 0)),
            pl.BlockSpec((1, width), lambda bi, ti: (0, 0)),
        ],
        out_specs=pl.BlockSpec((None, tb, width), lambda bi, ti: (bi, ti, 0)),
        scratch_shapes=[pltpu.VMEM((tb + POOL_MAX, width), F32)],
        compiler_params=_params(("parallel", "arbitrary")),
        name="causal_pool",
    )(p, pool_w, scale)


def kernel(x, lb_logits, norm_ffn1, ffn1_w_gate, ffn1_w_up, ffn1_w_down, norm_mix, w_in, gdn_conv_w, gdn_a_log, gdn_dt_bias, hgrn_norm_w, gdn_norm_w, pool_w, pool_scale, w_out, norm_ffn2, ffn2_w_gate, ffn2_w_up, ffn2_w_down, norm_final):
    bsz, seq, d = x.shape
    depth = lb_logits.shape[0]
    hgrn_w = lb_logits.shape[1]
    gdn_heads = gdn_a_log.shape[1]
    gdn_w = gdn_heads * HEAD_DIM
    pool_width = pool_scale.shape[1]
    hgrn_heads = hgrn_w // HEAD_DIM
    n = bsz * seq

    main = 4 * hgrn_w + 4 * gdn_w
    e_used = main + pool_width + LANES
    tn_in = 768
    e_pad = -(-e_used // tn_in) * tn_in
    w_in_r = _w_in_prep(w_in, main=main, nscal=2 * gdn_heads, e_pad=e_pad, tr=256)
    gdn_col0 = 4 * hgrn_heads
    pool_colblk = main // pool_width
    gm_col = (main + pool_width) // LANES

    lane_pad = lambda a: jnp.pad(a, ((0, 0), (gdn_heads, LANES - 2 * gdn_heads)))
    gdn_ap = jnp.stack([lane_pad(gdn_a_log), lane_pad(gdn_dt_bias)], axis=1)

    hgrn_wmat = jnp.asarray(_hgrn_decay_matrix(), BF16)
    tri = np.tril(np.ones((CHUNK, CHUNK), np.float32))
    tri3 = jnp.asarray(np.concatenate([tri, tri, tri], axis=1), BF16)

    lbp = _lb_params(lb_logits.astype(F32))

    w_out_b = w_out.astype(BF16)

    ffn1_w = [w.astype(BF16) for w in (ffn1_w_gate, ffn1_w_up, ffn1_w_down)]
    ffn2_w = [w.astype(BF16) for w in (ffn2_w_gate, ffn2_w_up, ffn2_w_down)]
    pool_w_b = pool_w.astype(BF16)

    xf = x.reshape(n, d)
    row = lambda a: a.reshape(1, -1)
    ffn_tiles = dict(tm=1024, tf=512, sub=512)
    for l in range(depth):
        xf = _ffn(xf, row(norm_ffn1[l]), *ffn1_w, row(norm_final), layer=l, final_norm=False, **ffn_tiles)
        p = _inproj(xf, row(norm_mix[l]), w_in_r, layer=l, tm=1024, tn=tn_in).reshape(bsz, seq, e_pad)
        ya = _hgrn(p, lbp[l], row(hgrn_norm_w[l]), hgrn_wmat, heads=hgrn_heads, tb=1024)
        gates = _gdn_gates(p, gdn_ap[l], tri3, heads=gdn_heads, gm_col=gm_col, tb=1024)
        yb = _gdn(p, gates, gdn_conv_w[l], row(gdn_norm_w[l]), heads=gdn_heads, col0=gdn_col0, tb=256, hp=8)
        yc = _pool(p, pool_w_b[l], row(pool_scale[l]), colblk=pool_colblk, tb=512)
        xf = _outproj(xf, ya.reshape(n, -1), yb.reshape(n, -1), yc.reshape(n, -1), w_out_b, layer=l, tm=512)
        xf = _ffn(xf, row(norm_ffn2[l]), *ffn2_w, row(norm_final), layer=l,
                  final_norm=(l == depth - 1), **ffn_tiles)
    return xf.reshape(bsz, seq, d)
```

```python
import functools

import numpy as np
import jax
import jax.numpy as jnp
from jax import lax
from jax.experimental import pallas as pl
from jax.experimental.pallas import tpu as pltpu

HEAD_DIM = 128
POOL_WINDOWS = (2, 4, 8, 16)
POOL_MAX = max(POOL_WINDOWS)
CONV_WIDTH = 4
NORM_EPS = 1e-6
L2_EPS = 1e-6

LANES = 128
SUBLANES = 8
VMEM_LIMIT_BYTES = 56 * 1024 * 1024

TILES = dict(
    ffn=dict(tm=1024, tf=512, sub=512),
    inproj=dict(tm=1024, tn=768),
    outproj=dict(tm=512),
    hgrn=dict(tb=2048),
    gdn_gates=dict(tb=1024),
    gdn=dict(tb=256, hp=8),
    pool=dict(tb=1024),
    w_in_prep=dict(tr=256),
)

CHUNK = 128
N_LEVELS = 7

F32 = jnp.float32
BF16 = jnp.bfloat16


def _params(sem):
    return pltpu.CompilerParams(dimension_semantics=sem, vmem_limit_bytes=VMEM_LIMIT_BYTES)


def _rms(x, w):
    return x * lax.rsqrt(jnp.mean(x * x, axis=-1, keepdims=True) + NORM_EPS) * w


def _silu(x):
    return x * jax.nn.sigmoid(x)


def _log1p_exp(x):
    return jnp.log(1.0 + jnp.exp(x))


def _dot(a, b):
    return jnp.dot(a.astype(BF16), b.astype(BF16), preferred_element_type=F32)


def _dot_nt(a, b):
    return lax.dot_general(a.astype(BF16), b.astype(BF16), (((1,), (1,)), ((), ())),
                           preferred_element_type=F32)


def _dot_tn(a, b):
    return lax.dot_general(a.astype(BF16), b.astype(BF16), (((0,), (0,)), ((), ())),
                           preferred_element_type=F32)


def _dot_f32(a, b):
    return jnp.dot(a, b, preferred_element_type=F32, precision=lax.Precision.HIGHEST)


def _pair_level(tt, uu):
    x = tt ^ uu
    hb = jnp.zeros(x.shape, jnp.int32)
    for k in range(1, N_LEVELS):
        hb = hb + (x >= (1 << k)).astype(jnp.int32)
    return jnp.where(tt > uu, hb, -1)


def _split3(x):
    hi = x.astype(BF16)
    r = x - hi.astype(F32)
    mid = r.astype(BF16)
    lo = (r - mid.astype(F32)).astype(BF16)
    return jnp.concatenate([hi, mid, lo], axis=0)


def _ffn_kernel(x_ref, nw_ref, wg_ref, wu_ref, wd_ref, fw_ref, o_ref, h_ref, *, final_norm, sub):
    f = pl.program_id(1)
    tm = x_ref.shape[0]

    @pl.when(f == 0)
    def _():
        x = x_ref[...]
        h_ref[...] = _rms(x, nw_ref[...]).astype(BF16)
        o_ref[...] = x

    for r in range(tm // sub):
        rows = slice(r * sub, (r + 1) * sub)
        h = h_ref[rows, :]
        g = jnp.dot(h, wg_ref[...], preferred_element_type=F32)
        u = jnp.dot(h, wu_ref[...], preferred_element_type=F32)
        a = (_silu(g) * (0.5 * u)).astype(BF16)
        o_ref[rows, :] += jnp.dot(a, wd_ref[...], preferred_element_type=F32)

    if final_norm:
        @pl.when(f == pl.num_programs(1) - 1)
        def _():
            o_ref[...] = _rms(o_ref[...], fw_ref[...])


def _ffn(x, nw, wg, wu, wd, fw, *, layer, final_norm, tm, tf, sub):
    n, d = x.shape
    dff = wg.shape[2]
    tm = min(tm, n)
    sub = min(sub, tm)
    return pl.pallas_call(
        functools.partial(_ffn_kernel, final_norm=final_norm, sub=sub),
        out_shape=jax.ShapeDtypeStruct((n, d), F32),
        grid=(n // tm, dff // tf),
        in_specs=[
            pl.BlockSpec((tm, d), lambda i, f: (i, 0)),
            pl.BlockSpec((1, d), lambda i, f: (0, 0)),
            pl.BlockSpec((None, d, tf), lambda i, f: (layer, 0, f)),
            pl.BlockSpec((None, d, tf), lambda i, f: (layer, 0, f)),
            pl.BlockSpec((None, tf, d), lambda i, f: (layer, f, 0)),
            pl.BlockSpec((1, d), lambda i, f: (0, 0)),
        ],
        out_specs=pl.BlockSpec((tm, d), lambda i, f: (i, 0)),
        scratch_shapes=[pltpu.VMEM((tm, d), BF16)],
        compiler_params=_params(("parallel", "arbitrary")),
        name="ffn",
    )(x, nw, wg, wu, wd, fw)


def _inproj_kernel(x_ref, nw_ref, w_ref, o_ref, h_ref):
    @pl.when(pl.program_id(1) == 0)
    def _():
        h_ref[...] = _rms(x_ref[...], nw_ref[...]).astype(BF16)

    o_ref[...] = jnp.dot(h_ref[...], w_ref[...], preferred_element_type=F32)


def _inproj(x, nw, w, *, layer, tm, tn):
    n, d = x.shape
    e = w.shape[2]
    tm = min(tm, n)
    return pl.pallas_call(
        _inproj_kernel,
        out_shape=jax.ShapeDtypeStruct((n, e), F32),
        grid=(n // tm, e // tn),
        in_specs=[
            pl.BlockSpec((tm, d), lambda i, j: (i, 0)),
            pl.BlockSpec((1, d), lambda i, j: (0, 0)),
            pl.BlockSpec((None, d, tn), lambda i, j: (layer, 0, j)),
        ],
        out_specs=pl.BlockSpec((tm, tn), lambda i, j: (i, j)),
        scratch_shapes=[pltpu.VMEM((tm, d), BF16)],
        compiler_params=_params(("parallel", "arbitrary")),
        name="inproj",
    )(x, nw, w)


def _w_in_prep_kernel(w_ref, o_ref, *, main, nscal):
    e = w_ref.shape[1]
    e_pad = o_ref.shape[1]
    rest = e - main - nscal
    o_ref[:, 0:main] = w_ref[:, 0:main].astype(BF16)
    o_ref[:, main:main + rest] = w_ref[:, main + nscal:e].astype(BF16)
    tail = jnp.concatenate([w_ref[:, main:main + nscal].astype(BF16),
                            jnp.zeros((w_ref.shape[0], e_pad - e), BF16)], axis=1)
    o_ref[:, main + rest:e_pad] = tail


def _w_in_prep(w_in, *, main, nscal, e_pad, tr):
    depth, d, e = w_in.shape
    rows = depth * d
    out = pl.pallas_call(
        functools.partial(_w_in_prep_kernel, main=main, nscal=nscal),
        out_shape=jax.ShapeDtypeStruct((rows, e_pad), BF16),
        grid=(rows // tr,),
        in_specs=[pl.BlockSpec((tr, e), lambda i: (i, 0))],
        out_specs=pl.BlockSpec((tr, e_pad), lambda i: (i, 0)),
        compiler_params=_params(("parallel",)),
        name="w_in_regroup",
    )(w_in.reshape(rows, e))
    return out.reshape(depth, d, e_pad)


def _outproj_kernel(x_ref, ya_ref, yb_ref, yc_ref, w_ref, o_ref):
    na, nb = ya_ref.shape[1], yb_ref.shape[1]
    acc = jnp.dot(ya_ref[...], w_ref[0:na, :], preferred_element_type=F32)
    acc += jnp.dot(yb_ref[...], w_ref[na:na + nb, :], preferred_element_type=F32)
    acc += jnp.dot(yc_ref[...], w_ref[na + nb:, :], preferred_element_type=F32)
    o_ref[...] = x_ref[...] + acc


def _outproj(x, ya, yb, yc, w, *, layer, tm):
    n, d = x.shape
    tm = min(tm, n)
    row = lambda a: pl.BlockSpec((tm, a.shape[1]), lambda i: (i, 0))
    return pl.pallas_call(
        _outproj_kernel,
        out_shape=jax.ShapeDtypeStruct((n, d), F32),
        grid=(n // tm,),
        in_specs=[row(x), row(ya), row(yb), row(yc),
                  pl.BlockSpec((None,) + w.shape[1:], lambda i: (layer, 0, 0))],
        out_specs=row(x),
        compiler_params=_params(("parallel",)),
        name="outproj",
    )(x, ya, yb, yc, w)


def _lb_kernel(x_ref, o_ref):
    depth = x_ref.shape[0]
    x = x_ref[...]
    e = jnp.exp(x - jnp.max(x, axis=0, keepdims=True))
    sm = e / jnp.sum(e, axis=0, keepdims=True)
    run = sm[0:1]
    first = run
    for l in range(depth):
        if l > 0:
            run = run + sm[l:l + 1]
        lb = run - first
        o_ref[l] = jnp.concatenate([jnp.log(lb), jnp.log1p(-lb), 1.0 - lb], axis=0)


def _lb_params(lb_logits):
    depth, w = lb_logits.shape
    return pl.pallas_call(
        _lb_kernel,
        out_shape=jax.ShapeDtypeStruct((depth, 3, w), F32),
        name="hgrn_lower_bounds",
    )(lb_logits)


def _hgrn_decay_matrix():
    c = CHUNK
    t = np.arange(c)[:, None]
    u = np.arange(c)[None, :]
    blocks = [(u <= t)]
    for k in range(N_LEVELS - 1, 0, -1):
        s = 1 << k
        ref = (t // (2 * s)) * (2 * s) + s - 1
        odd = (t & s) != 0
        blocks.append(np.where(odd, (u > ref) & (u <= t), (u > t) & (u <= ref)))
    w = np.concatenate(blocks, axis=0).astype(np.float32)
    return np.concatenate([w, w], axis=1)


def _hgrn_kernel(q_ref, f_ref, i_ref, g_ref, lb_ref, nw_ref, w_ref, o_ref,
                 st_ref, oraw_ref, qc_ref, r_ref, el_ref):
    c = CHUNK
    tb = q_ref.shape[0]

    @pl.when(pl.program_id(2) == 0)
    def _():
        st_ref[...] = jnp.zeros_like(st_ref)

    log_lb = lb_ref[0:1, :]
    log_1m = lb_ref[1:2, :]
    one_m = lb_ref[2:3, :]
    nw = nw_ref[...]

    row = lax.broadcasted_iota(jnp.int32, (c, 1), 0)
    tt = lax.broadcasted_iota(jnp.int32, (c, c), 0)
    uu = lax.broadcasted_iota(jnp.int32, (c, c), 1)
    hb = _pair_level(tt, uu)
    odd_of = lambda k: (row & (1 << k)) != 0

    n_chunks = tb // c
    assert n_chunks % 2 == 0
    rows_of = lambda ci: slice(ci * c, (ci + 1) * c)
    qhs, kks, lfs, his, los = [], [], [], [], []
    for ci in range(n_chunks):
        rows = rows_of(ci)
        z = f_ref[rows, :]
        y = log_1m + (jnp.minimum(z, 0.0) - _log1p_exp(-jnp.abs(z)))
        log_f = jnp.maximum(log_lb, y) + _log1p_exp(-jnp.abs(log_lb - y))
        hi = log_f.astype(BF16)
        qhs.append(_silu(q_ref[rows, :]))
        kks.append(one_m * jax.nn.sigmoid(-z))
        lfs.append(log_f)
        his.append(hi)
        los.append((log_f - hi.astype(F32)).astype(BF16))

    dalls = []
    for a in range(0, n_chunks, 2):
        rhs = jnp.concatenate([jnp.concatenate([his[a], his[a + 1]], axis=1),
                               jnp.concatenate([los[a], los[a + 1]], axis=1)], axis=0)
        d = jnp.dot(w_ref[...], rhs, preferred_element_type=F32)
        dalls += [d[:, 0:HEAD_DIM], d[:, HEAD_DIM:2 * HEAD_DIM]]

    ps = [jnp.zeros((c, c), F32) for _ in range(n_chunks)]
    for k in range(N_LEVELS - 1, -1, -1):
        odd = odd_of(k)
        for ci in range(n_chunks):
            li = N_LEVELS - 1 - k
            expo = dalls[ci][(1 + li) * c:(2 + li) * c] if k > 0 else jnp.where(odd, lfs[ci], 0.0)
            xs = (jnp.where(odd, qhs[ci], kks[ci]) * jnp.exp(expo)).astype(BF16)
            pk = lax.dot_general(xs, xs, (((1,), (1,)), ((), ())), preferred_element_type=F32)
            ps[ci] = jnp.where(hb == k, pk, ps[ci])

    for ci in range(n_chunks):
        rows = rows_of(ci)
        v = i_ref[rows, :]
        cum = dalls[ci][0:c]
        last = cum[c - 1:c, :]
        diag = jnp.sum(qhs[ci] * kks[ci], axis=-1, keepdims=True)
        oraw_ref[rows, :] = _dot(ps[ci], v) + diag * v
        qc_ref[rows, :] = (qhs[ci] * jnp.exp(cum)).astype(BF16)
        r_ref[rows, :] = _dot_tn(v, kks[ci] * jnp.exp(last - cum))
        el_ref[ci * SUBLANES:(ci + 1) * SUBLANES, :] = jnp.broadcast_to(jnp.exp(last), (SUBLANES, LANES))

    st = st_ref[...]
    for ci in range(n_chunks):
        rows = rows_of(ci)
        oraw_ref[rows, :] += _dot_nt(qc_ref[rows, :], st)
        st = st * el_ref[ci * SUBLANES:ci * SUBLANES + 1, :] + r_ref[rows, :]
    st_ref[...] = st

    o_ref[...] = (_rms(oraw_ref[...], nw) * _silu(g_ref[...])).astype(o_ref.dtype)


def _hgrn(p, lbp, nw, wmat, *, heads, tb):
    b, t, _ = p.shape
    tb = min(tb, t)
    col = lambda off: pl.BlockSpec((None, tb, LANES), lambda bi, h, ti: (bi, ti, off + h))
    return pl.pallas_call(
        _hgrn_kernel,
        out_shape=jax.ShapeDtypeStruct((b, t, heads * HEAD_DIM), BF16),
        grid=(b, heads, t // tb),
        in_specs=[
            col(0), col(heads), col(2 * heads), col(3 * heads),
            pl.BlockSpec((3, LANES), lambda bi, h, ti: (0, h)),
            pl.BlockSpec((1, LANES), lambda bi, h, ti: (0, 0)),
            pl.BlockSpec(wmat.shape, lambda bi, h, ti: (0, 0)),
        ],
        out_specs=pl.BlockSpec((None, tb, LANES), lambda bi, h, ti: (bi, ti, h)),
        scratch_shapes=[
            pltpu.VMEM((HEAD_DIM, HEAD_DIM), F32),
            pltpu.VMEM((tb, HEAD_DIM), F32),
            pltpu.VMEM((tb, HEAD_DIM), BF16),
            pltpu.VMEM((tb, HEAD_DIM), F32),
            pltpu.VMEM((tb // CHUNK * SUBLANES, LANES), F32),
        ],
        compiler_params=_params(("parallel", "parallel", "arbitrary")),
        name="hgrn2",
    )(p, p, p, p, lbp, nw, wmat)


def _gdn_gates_kernel(gm_ref, ap_ref, tri_ref, o_ref, *, heads):
    c = CHUNK
    gm = gm_ref[...]
    xg = gm + ap_ref[1:2, :]
    g_all = -jnp.exp(ap_ref[0:1, :]) * (jnp.maximum(xg, 0.0) + _log1p_exp(-jnp.abs(xg)))
    lane = lax.broadcasted_iota(jnp.int32, (1, LANES), 1)
    beta = jax.nn.sigmoid(gm)
    for ci in range(gm.shape[0] // c):
        rows = slice(ci * c, (ci + 1) * c)
        gcum = jnp.dot(tri_ref[...], _split3(g_all[rows]), preferred_element_type=F32)
        o_ref[rows, :] = jnp.where(lane < heads, beta[rows], gcum)


def _gdn_gates(p, ap, tri, *, heads, gm_col, tb):
    b, t, _ = p.shape
    tb = min(tb, t)
    return pl.pallas_call(
        functools.partial(_gdn_gates_kernel, heads=heads),
        out_shape=jax.ShapeDtypeStruct((b, t, LANES), F32),
        grid=(b, t // tb),
        in_specs=[
            pl.BlockSpec((None, tb, LANES), lambda bi, ti: (bi, ti, gm_col)),
            pl.BlockSpec(ap.shape, lambda bi, ti: (0, 0)),
            pl.BlockSpec(tri.shape, lambda bi, ti: (0, 0)),
        ],
        out_specs=pl.BlockSpec((None, tb, LANES), lambda bi, ti: (bi, ti, 0)),
        compiler_params=_params(("parallel", "parallel")),
        name="gdn_gates",
    )(p, ap, tri)


def _gdn_kernel(q_ref, k_ref, v_ref, g_ref, gm_ref, cq_ref, ck_ref, cv_ref, nw_ref,
                o_ref, s_ref, halo_ref, ext_ref, qn_ref, kn_ref, vn_ref,
                qd_ref, kd_ref, at_ref, gl_ref, r_ref, pq_ref, oraw_ref, *, head_off, hp):
    c = CHUNK
    tb = q_ref.shape[0]
    h0 = pl.program_id(1) * hp
    pad = SUBLANES

    @pl.when(pl.program_id(2) == 0)
    def _():
        s_ref[...] = jnp.zeros_like(s_ref)
        halo_ref[...] = jnp.zeros_like(halo_ref)

    def conv(idx, src_ref, cw_ref):
        ext_ref[0:pad, :] = halo_ref[idx]
        ext_ref[pad:pad + tb, :] = src_ref[...]
        halo_ref[idx] = src_ref[tb - pad:tb, :]
        acc = cw_ref[CONV_WIDTH - 1:CONV_WIDTH, :] * ext_ref[pad:pad + tb, :]
        for j in range(CONV_WIDTH - 1):
            sh = CONV_WIDTH - 1 - j
            acc = acc + cw_ref[j:j + 1, :] * ext_ref[pad - sh:pad - sh + tb, :]
        return _silu(acc)

    def l2n(a):
        return a * lax.rsqrt(jnp.sum(a * a, axis=-1, keepdims=True) + L2_EPS)

    qc, kc = conv(0, q_ref, cq_ref), conv(1, k_ref, ck_ref)
    vn_ref[...] = conv(2, v_ref, cv_ref)
    for hh in range(hp):
        ln = slice(hh * HEAD_DIM, (hh + 1) * HEAD_DIM)
        qn_ref[:, ln] = l2n(qc[:, ln]) * (HEAD_DIM ** -0.5)
        kn_ref[:, ln] = l2n(kc[:, ln])

    nw = nw_ref[...]
    lane = lax.broadcasted_iota(jnp.int32, (1, LANES), 1)
    tt = lax.broadcasted_iota(jnp.int32, (c, c), 0)
    uu = lax.broadcasted_iota(jnp.int32, (c, c), 1)
    hb = _pair_level(tt, uu)

    n_chunks = tb // c
    units = [(hh, ci) for ci in range(n_chunks) for hh in range(hp)]
    rows_of = lambda ci: slice(ci * c, (ci + 1) * c)
    lanes_of = lambda hh: slice(hh * HEAD_DIM, (hh + 1) * HEAD_DIM)
    gl_rows = lambda hh, ci: (hh * n_chunks + ci) * SUBLANES
    nbs, rhss = [], []
    for hh, ci in units:
        rows, ln = rows_of(ci), lanes_of(hh)
        q = qn_ref[rows, ln]
        k = kn_ref[rows, ln]
        v = vn_ref[rows, ln]
        gm = gm_ref[rows, :]

        beta = jnp.sum(gm * (lane == h0 + hh).astype(F32), axis=-1, keepdims=True)
        gcum = jnp.sum(gm * (lane == h0 + hh + head_off).astype(F32), axis=-1, keepdims=True)
        g_last = gcum[c - 1:c, :]

        gb = jnp.broadcast_to(gcum, (c, c))
        decay = jnp.exp(jnp.where(tt >= uu, gb - gb.T, -jnp.inf))
        kb = k * beta
        m = _dot_nt(jnp.concatenate([kb, q], axis=0), k)
        nbs.append((-(m[0:c] * decay)).astype(BF16))

        e_g = jnp.exp(gcum)
        rhss.append(jnp.concatenate([v * beta, kb * e_g], axis=1))
        qd_ref[rows, ln] = q * e_g
        kd_ref[rows, ln] = (k * jnp.exp(g_last - gcum)).astype(BF16)
        at_ref[hh, rows, :] = (m[c:2 * c] * decay).astype(BF16)
        gl_ref[gl_rows(hh, ci):gl_rows(hh, ci) + SUBLANES, :] = jnp.broadcast_to(jnp.exp(g_last), (SUBLANES, LANES))

    mis = [jnp.where(hb == 0, nb, 0).astype(F32) for nb in nbs]
    one_b = jnp.ones((c, c), BF16)
    for lv in range(1, N_LEVELS):
        o_lvs = [jnp.where(hb == lv, nb, 0) for nb in nbs]
        mbs = [mi.astype(BF16) for mi in mis]
        ys = [jnp.dot(jnp.where(tt == uu, one_b, mb), o_lv, preferred_element_type=F32)
              for mb, o_lv in zip(mbs, o_lvs)]
        mis = [mi + y + jnp.dot(y.astype(BF16), mb, preferred_element_type=F32)
               for mi, y, mb in zip(mis, ys, mbs)]

    sols = [rhs + _dot(mi, rhs) for mi, rhs in zip(mis, rhss)]
    for (hh, ci), sol in zip(units, sols):
        rows, ln = rows_of(ci), lanes_of(hh)
        sol = sol.astype(BF16)
        kd_uw = _dot_tn(kd_ref[rows, ln], sol)
        at_uw = jnp.dot(at_ref[hh, rows, :], sol, preferred_element_type=F32)
        r_ref[hh, rows, :] = kd_uw[:, 0:HEAD_DIM]
        pq_ref[hh, ci, 0:c, :] = kd_uw[:, HEAD_DIM:2 * HEAD_DIM].astype(BF16)
        pq_ref[hh, ci, c:2 * c, :] = (qd_ref[rows, ln] - at_uw[:, HEAD_DIM:2 * HEAD_DIM]).astype(BF16)
        oraw_ref[rows, ln] = at_uw[:, 0:HEAD_DIM]

    ss = [s_ref[hh] for hh in range(hp)]
    for ci in range(n_chunks):
        rows = rows_of(ci)
        zs = [jnp.dot(pq_ref[hh, ci], ss[hh].astype(BF16), preferred_element_type=F32) for hh in range(hp)]
        for hh in range(hp):
            oraw_ref[rows, lanes_of(hh)] += zs[hh][c:2 * c]
            ss[hh] = (ss[hh] * gl_ref[gl_rows(hh, ci):gl_rows(hh, ci) + 1, :] - zs[hh][0:c]
                      + r_ref[hh, rows, :])
    for hh in range(hp):
        s_ref[hh] = ss[hh]

    g = _silu(g_ref[...])
    for hh in range(hp):
        ln = lanes_of(hh)
        o_ref[:, ln] = (_rms(oraw_ref[:, ln], nw) * g[:, ln]).astype(o_ref.dtype)


def _gdn(p, gates, conv_w, nw, *, heads, col0, tb, hp):
    b, t, _ = p.shape
    tb = min(tb, t)
    w = hp * HEAD_DIM
    n_chunks = tb // CHUNK
    col = lambda off: pl.BlockSpec((None, tb, w), lambda bi, h, ti: (bi, ti, off // hp + h))
    cw = lambda off: pl.BlockSpec((CONV_WIDTH, w), lambda bi, h, ti: (0, off // hp + h))
    return pl.pallas_call(
        functools.partial(_gdn_kernel, head_off=heads, hp=hp),
        out_shape=jax.ShapeDtypeStruct((b, t, heads * HEAD_DIM), BF16),
        grid=(b, heads // hp, t // tb),
        in_specs=[
            col(col0), col(col0 + heads), col(col0 + 2 * heads), col(col0 + 3 * heads),
            pl.BlockSpec((None, tb, LANES), lambda bi, h, ti: (bi, ti, 0)),
            cw(0), cw(heads), cw(2 * heads),
            pl.BlockSpec(nw.shape, lambda bi, h, ti: (0, 0)),
        ],
        out_specs=pl.BlockSpec((None, tb, w), lambda bi, h, ti: (bi, ti, h)),
        scratch_shapes=[
            pltpu.VMEM((hp, HEAD_DIM, HEAD_DIM), F32),
            pltpu.VMEM((3, SUBLANES, w), F32),
            pltpu.VMEM((tb + SUBLANES, w), F32),
            pltpu.VMEM((tb, w), F32),
            pltpu.VMEM((tb, w), F32),
            pltpu.VMEM((tb, w), F32),
            pltpu.VMEM((tb, w), F32),
            pltpu.VMEM((tb, w), BF16),
            pltpu.VMEM((hp, tb, CHUNK), BF16),
            pltpu.VMEM((hp * n_chunks * SUBLANES, LANES), F32),
            pltpu.VMEM((hp, tb, HEAD_DIM), F32),
            pltpu.VMEM((hp, n_chunks, 2 * CHUNK, HEAD_DIM), BF16),
            pltpu.VMEM((tb, w), F32),
        ],
        compiler_params=_params(("parallel", "parallel", "arbitrary")),
        name="gated_deltanet",
    )(p, p, p, p, gates, conv_w, conv_w, conv_w, nw)


def _pool_kernel(u_ref, w_ref, sc_ref, o_ref, ext_ref):
    tb = u_ref.shape[0]
    ti = pl.program_id(1)

    @pl.when(ti == 0)
    def _():
        ext_ref[0:POOL_MAX, :] = jnp.zeros((POOL_MAX, ext_ref.shape[1]), F32)

    ext_ref[POOL_MAX:POOL_MAX + tb, :] = u_ref[...]
    pos = ti * tb + lax.broadcasted_iota(jnp.int32, (tb, 1), 0)
    for gi, win in enumerate(POOL_WINDOWS):
        sl = slice(gi * LANES, (gi + 1) * LANES)
        cur = ext_ref[POOL_MAX:POOL_MAX + tb, sl]
        ws = cur
        for j in range(1, win):
            ws = ws + ext_ref[POOL_MAX - j:POOL_MAX - j + tb, sl]
        count = jnp.minimum(pos + 1, win).astype(F32)
        m = ws / count - cur
        o_ref[:, sl] = (_dot(m, w_ref[gi]) * sc_ref[:, sl]).astype(o_ref.dtype)
    ext_ref[0:POOL_MAX, :] = ext_ref[tb:tb + POOL_MAX, :]


def _pool(p, pool_w, scale, *, colblk, tb):
    b, t, _ = p.shape
    tb = min(tb, t)
    width = len(POOL_WINDOWS) * LANES
    return pl.pallas_call(
        _pool_kernel,
        out_shape=jax.ShapeDtypeStruct((b, t, width), BF16),
        grid=(b, t // tb),
        in_specs=[
            pl.BlockSpec((None, tb, width), lambda bi, ti: (bi, ti, colblk)),
            pl.BlockSpec(pool_w.shape, lambda bi, ti: (0, 0, 0)),
            pl.BlockSpec((1, width), lambda bi, ti: (0, 0)),
        ],
        out_specs=pl.BlockSpec((None, tb, width), lambda bi, ti: (bi, ti, 0)),
        scratch_shapes=[pltpu.VMEM((tb + POOL_MAX, width), F32)],
        compiler_params=_params(("parallel", "arbitrary")),
        name="causal_pool",
    )(p, pool_w, scale)


def kernel(x, lb_logits, norm_ffn1, ffn1_w_gate, ffn1_w_up, ffn1_w_down, norm_mix, w_in, gdn_conv_w, gdn_a_log, gdn_dt_bias, hgrn_norm_w, gdn_norm_w, pool_w, pool_scale, w_out, norm_ffn2, ffn2_w_gate, ffn2_w_up, ffn2_w_down, norm_final):
    bsz, seq, d = x.shape
    depth = lb_logits.shape[0]
    hgrn_w = lb_logits.shape[1]
    gdn_heads = gdn_a_log.shape[1]
    gdn_w = gdn_heads * HEAD_DIM
    pool_width = pool_scale.shape[1]
    hgrn_heads = hgrn_w // HEAD_DIM
    n = bsz * seq

    main = 4 * hgrn_w + 4 * gdn_w
    e_used = main + pool_width + LANES
    tn_in = TILES["inproj"]["tn"]
    e_pad = -(-e_used // tn_in) * tn_in
    w_in_r = _w_in_prep(w_in, main=main, nscal=2 * gdn_heads, e_pad=e_pad, **TILES["w_in_prep"])
    gdn_col0 = 4 * hgrn_heads
    pool_colblk = main // pool_width
    gm_col = (main + pool_width) // LANES

    lane_pad = lambda a: jnp.pad(a, ((0, 0), (gdn_heads, LANES - 2 * gdn_heads)))
    gdn_ap = jnp.stack([lane_pad(gdn_a_log), lane_pad(gdn_dt_bias)], axis=1)

    hgrn_wmat = jnp.asarray(_hgrn_decay_matrix(), BF16)
    tri = np.tril(np.ones((CHUNK, CHUNK), np.float32))
    tri3 = jnp.asarray(np.concatenate([tri, tri, tri], axis=1), BF16)

    lbp = _lb_params(lb_logits.astype(F32))

    w_out_b = w_out.astype(BF16)

    ffn1_w = [w.astype(BF16) for w in (ffn1_w_gate, ffn1_w_up, ffn1_w_down)]
    ffn2_w = [w.astype(BF16) for w in (ffn2_w_gate, ffn2_w_up, ffn2_w_down)]
    pool_w_b = pool_w.astype(BF16)

    xf = x.reshape(n, d)
    row = lambda a: a.reshape(1, -1)
    for l in range(depth):
        xf = _ffn(xf, row(norm_ffn1[l]), *ffn1_w, row(norm_final), layer=l, final_norm=False, **TILES["ffn"])
        p = _inproj(xf, row(norm_mix[l]), w_in_r, layer=l, **TILES["inproj"]).reshape(bsz, seq, e_pad)
        ya = _hgrn(p, lbp[l], row(hgrn_norm_w[l]), hgrn_wmat, heads=hgrn_heads, **TILES["hgrn"])
        gates = _gdn_gates(p, gdn_ap[l], tri3, heads=gdn_heads, gm_col=gm_col, **TILES["gdn_gates"])
        yb = _gdn(p, gates, gdn_conv_w[l], row(gdn_norm_w[l]), heads=gdn_heads, col0=gdn_col0, **TILES["gdn"])
        yc = _pool(p, pool_w_b[l], row(pool_scale[l]), colblk=pool_colblk, **TILES["pool"])
        xf = _outproj(xf, ya.reshape(n, -1), yb.reshape(n, -1), yc.reshape(n, -1), w_out_b, layer=l,
                      **TILES["outproj"])
        xf = _ffn(xf, row(norm_ffn2[l]), *ffn2_w, row(norm_final), layer=l,
                  final_norm=(l == depth - 1), **TILES["ffn"])
    return xf.reshape(bsz, seq, d)
```

```python
import functools

import numpy as np
import jax
import jax.numpy as jnp
from jax import lax
from jax.experimental import pallas as pl
from jax.experimental.pallas import tpu as pltpu

HEAD_DIM = 128
POOL_WINDOWS = (2, 4, 8, 16)
POOL_MAX = max(POOL_WINDOWS)
CONV_WIDTH = 4
NORM_EPS = 1e-6
L2_EPS = 1e-6

LANES = 128
SUBLANES = 8
VMEM_LIMIT_BYTES = 56 * 1024 * 1024

TILES = dict(
    ffn=dict(tm=1024, tf=512, sub=512),
    inproj=dict(tm=1024, tn=768),
    outproj=dict(tm=512),
    hgrn=dict(tb=2048),
    gdn_gates=dict(tb=1024),
    gdn=dict(tb=256, hp=8),
    pool=dict(tb=1024),
    w_in_prep=dict(tr=256),
)

CHUNK = 128
N_LEVELS = 7

F32 = jnp.float32
BF16 = jnp.bfloat16


def _params(sem):
    return pltpu.CompilerParams(dimension_semantics=sem, vmem_limit_bytes=VMEM_LIMIT_BYTES)


def _rms(x, w):
    return x * lax.rsqrt(jnp.mean(x * x, axis=-1, keepdims=True) + NORM_EPS) * w


def _silu(x):
    return x * jax.nn.sigmoid(x)


def _log1p_exp(x):
    return jnp.log(1.0 + jnp.exp(x))


def _dot(a, b):
    return jnp.dot(a.astype(BF16), b.astype(BF16), preferred_element_type=F32)


def _dot_nt(a, b):
    return lax.dot_general(a.astype(BF16), b.astype(BF16), (((1,), (1,)), ((), ())),
                           preferred_element_type=F32)


def _dot_tn(a, b):
    return lax.dot_general(a.astype(BF16), b.astype(BF16), (((0,), (0,)), ((), ())),
                           preferred_element_type=F32)


def _dot_f32(a, b):
    return jnp.dot(a, b, preferred_element_type=F32, precision=lax.Precision.HIGHEST)


def _pair_level(tt, uu):
    x = tt ^ uu
    hb = jnp.zeros(x.shape, jnp.int32)
    for k in range(1, N_LEVELS):
        hb = hb + (x >= (1 << k)).astype(jnp.int32)
    return jnp.where(tt > uu, hb, -1)


def _split3(x):
    hi = x.astype(BF16)
    r = x - hi.astype(F32)
    mid = r.astype(BF16)
    lo = (r - mid.astype(F32)).astype(BF16)
    return jnp.concatenate([hi, mid, lo], axis=0)


def _ffn_kernel(x_ref, nw_ref, wg_ref, wu_ref, wd_ref, fw_ref, o_ref, h_ref, *, final_norm, sub):
    f = pl.program_id(1)
    tm = x_ref.shape[0]

    @pl.when(f == 0)
    def _():
        x = x_ref[...]
        h_ref[...] = _rms(x, nw_ref[...]).astype(BF16)
        o_ref[...] = x

    for r in range(tm // sub):
        rows = slice(r * sub, (r + 1) * sub)
        h = h_ref[rows, :]
        g = jnp.dot(h, wg_ref[...], preferred_element_type=F32)
        u = jnp.dot(h, wu_ref[...], preferred_element_type=F32)
        a = (_silu(g) * (0.5 * u)).astype(BF16)
        o_ref[rows, :] += jnp.dot(a, wd_ref[...], preferred_element_type=F32)

    if final_norm:
        @pl.when(f == pl.num_programs(1) - 1)
        def _():
            o_ref[...] = _rms(o_ref[...], fw_ref[...])


def _ffn(x, nw, wg, wu, wd, fw, *, layer, final_norm, tm, tf, sub):
    n, d = x.shape
    dff = wg.shape[2]
    tm = min(tm, n)
    sub = min(sub, tm)
    return pl.pallas_call(
        functools.partial(_ffn_kernel, final_norm=final_norm, sub=sub),
        out_shape=jax.ShapeDtypeStruct((n, d), F32),
        grid=(n // tm, dff // tf),
        in_specs=[
            pl.BlockSpec((tm, d), lambda i, f: (i, 0)),
            pl.BlockSpec((1, d), lambda i, f: (0, 0)),
            pl.BlockSpec((None, d, tf), lambda i, f: (layer, 0, f)),
            pl.BlockSpec((None, d, tf), lambda i, f: (layer, 0, f)),
            pl.BlockSpec((None, tf, d), lambda i, f: (layer, f, 0)),
            pl.BlockSpec((1, d), lambda i, f: (0, 0)),
        ],
        out_specs=pl.BlockSpec((tm, d), lambda i, f: (i, 0)),
        scratch_shapes=[pltpu.VMEM((tm, d), BF16)],
        compiler_params=_params(("parallel", "arbitrary")),
        name="ffn",
    )(x, nw, wg, wu, wd, fw)


def _inproj_kernel(x_ref, nw_ref, w_ref, o_ref, h_ref):
    @pl.when(pl.program_id(1) == 0)
    def _():
        h_ref[...] = _rms(x_ref[...], nw_ref[...]).astype(BF16)

    o_ref[...] = jnp.dot(h_ref[...], w_ref[...], preferred_element_type=F32)


def _inproj(x, nw, w, *, layer, tm, tn):
    n, d = x.shape
    e = w.shape[2]
    tm = min(tm, n)
    return pl.pallas_call(
        _inproj_kernel,
        out_shape=jax.ShapeDtypeStruct((n, e), F32),
        grid=(n // tm, e // tn),
        in_specs=[
            pl.BlockSpec((tm, d), lambda i, j: (i, 0)),
            pl.BlockSpec((1, d), lambda i, j: (0, 0)),
            pl.BlockSpec((None, d, tn), lambda i, j: (layer, 0, j)),
        ],
        out_specs=pl.BlockSpec((tm, tn), lambda i, j: (i, j)),
        scratch_shapes=[pltpu.VMEM((tm, d), BF16)],
        compiler_params=_params(("parallel", "arbitrary")),
        name="inproj",
    )(x, nw, w)


def _w_in_prep_kernel(w_ref, o_ref, *, main, nscal):
    e = w_ref.shape[1]
    e_pad = o_ref.shape[1]
    rest = e - main - nscal
    o_ref[:, 0:main] = w_ref[:, 0:main].astype(BF16)
    o_ref[:, main:main + rest] = w_ref[:, main + nscal:e].astype(BF16)
    tail = jnp.concatenate([w_ref[:, main:main + nscal].astype(BF16),
                            jnp.zeros((w_ref.shape[0], e_pad - e), BF16)], axis=1)
    o_ref[:, main + rest:e_pad] = tail


def _w_in_prep(w_in, *, main, nscal, e_pad, tr):
    depth, d, e = w_in.shape
    rows = depth * d
    out = pl.pallas_call(
        functools.partial(_w_in_prep_kernel, main=main, nscal=nscal),
        out_shape=jax.ShapeDtypeStruct((rows, e_pad), BF16),
        grid=(rows // tr,),
        in_specs=[pl.BlockSpec((tr, e), lambda i: (i, 0))],
        out_specs=pl.BlockSpec((tr, e_pad), lambda i: (i, 0)),
        compiler_params=_params(("parallel",)),
        name="w_in_regroup",
    )(w_in.reshape(rows, e))
    return out.reshape(depth, d, e_pad)


def _outproj_kernel(x_ref, ya_ref, yb_ref, yc_ref, w_ref, o_ref):
    na, nb = ya_ref.shape[1], yb_ref.shape[1]
    acc = jnp.dot(ya_ref[...], w_ref[0:na, :], preferred_element_type=F32)
    acc += jnp.dot(yb_ref[...], w_ref[na:na + nb, :], preferred_element_type=F32)
    acc += jnp.dot(yc_ref[...], w_ref[na + nb:, :], preferred_element_type=F32)
    o_ref[...] = x_ref[...] + acc


def _outproj(x, ya, yb, yc, w, *, layer, tm):
    n, d = x.shape
    tm = min(tm, n)
    row = lambda a: pl.BlockSpec((tm, a.shape[1]), lambda i: (i, 0))
    return pl.pallas_call(
        _outproj_kernel,
        out_shape=jax.ShapeDtypeStruct((n, d), F32),
        grid=(n // tm,),
        in_specs=[row(x), row(ya), row(yb), row(yc),
                  pl.BlockSpec((None,) + w.shape[1:], lambda i: (layer, 0, 0))],
        out_specs=row(x),
        compiler_params=_params(("parallel",)),
        name="outproj",
    )(x, ya, yb, yc, w)


def _lb_kernel(x_ref, o_ref):
    depth = x_ref.shape[0]
    x = x_ref[...]
    e = jnp.exp(x - jnp.max(x, axis=0, keepdims=True))
    sm = e / jnp.sum(e, axis=0, keepdims=True)
    run = sm[0:1]
    first = run
    for l in range(depth):
        if l > 0:
            run = run + sm[l:l + 1]
        lb = run - first
        o_ref[l] = jnp.concatenate([jnp.log(lb), jnp.log1p(-lb), 1.0 - lb], axis=0)


def _lb_params(lb_logits):
    depth, w = lb_logits.shape
    return pl.pallas_call(
        _lb_kernel,
        out_shape=jax.ShapeDtypeStruct((depth, 3, w), F32),
        name="hgrn_lower_bounds",
    )(lb_logits)


def _hgrn_decay_matrix():
    c = CHUNK
    t = np.arange(c)[:, None]
    u = np.arange(c)[None, :]
    blocks = [(u <= t)]
    for k in range(N_LEVELS - 1, 0, -1):
        s = 1 << k
        ref = (t // (2 * s)) * (2 * s) + s - 1
        odd = (t & s) != 0
        blocks.append(np.where(odd, (u > ref) & (u <= t), (u > t) & (u <= ref)))
    w = np.concatenate(blocks, axis=0).astype(np.float32)
    return np.concatenate([w, w], axis=1)


def _hgrn_kernel(q_ref, f_ref, i_ref, g_ref, lb_ref, nw_ref, w_ref, o_ref,
                 st_ref, oraw_ref, qc_ref, r_ref, el_ref):
    c = CHUNK
    tb = q_ref.shape[0]

    @pl.when(pl.program_id(2) == 0)
    def _():
        st_ref[...] = jnp.zeros_like(st_ref)

    log_lb = lb_ref[0:1, :]
    log_1m = lb_ref[1:2, :]
    one_m = lb_ref[2:3, :]
    nw = nw_ref[...]

    row = lax.broadcasted_iota(jnp.int32, (c, 1), 0)
    tt = lax.broadcasted_iota(jnp.int32, (c, c), 0)
    uu = lax.broadcasted_iota(jnp.int32, (c, c), 1)
    hb = _pair_level(tt, uu)
    odd_of = lambda k: (row & (1 << k)) != 0

    n_chunks = tb // c
    assert n_chunks % 2 == 0
    rows_of = lambda ci: slice(ci * c, (ci + 1) * c)
    qhs, kks, lfs, his, los = [], [], [], [], []
    for ci in range(n_chunks):
        rows = rows_of(ci)
        z = f_ref[rows, :]
        y = log_1m + (jnp.minimum(z, 0.0) - _log1p_exp(-jnp.abs(z)))
        log_f = jnp.maximum(log_lb, y) + _log1p_exp(-jnp.abs(log_lb - y))
        hi = log_f.astype(BF16)
        qhs.append(_silu(q_ref[rows, :]))
        kks.append(one_m * jax.nn.sigmoid(-z))
        lfs.append(log_f)
        his.append(hi)
        los.append((log_f - hi.astype(F32)).astype(BF16))

    dalls = []
    for a in range(0, n_chunks, 2):
        rhs = jnp.concatenate([jnp.concatenate([his[a], his[a + 1]], axis=1),
                               jnp.concatenate([los[a], los[a + 1]], axis=1)], axis=0)
        d = jnp.dot(w_ref[...], rhs, preferred_element_type=F32)
        dalls += [d[:, 0:HEAD_DIM], d[:, HEAD_DIM:2 * HEAD_DIM]]

    ps = [jnp.zeros((c, c), F32) for _ in range(n_chunks)]
    for k in range(N_LEVELS - 1, -1, -1):
        odd = odd_of(k)
        for ci in range(n_chunks):
            li = N_LEVELS - 1 - k
            expo = dalls[ci][(1 + li) * c:(2 + li) * c] if k > 0 else jnp.where(odd, lfs[ci], 0.0)
            xs = (jnp.where(odd, qhs[ci], kks[ci]) * jnp.exp(expo)).astype(BF16)
            pk = lax.dot_general(xs, xs, (((1,), (1,)), ((), ())), preferred_element_type=F32)
            ps[ci] = jnp.where(hb == k, pk, ps[ci])

    for ci in range(n_chunks):
        rows = rows_of(ci)
        v = i_ref[rows, :]
        cum = dalls[ci][0:c]
        last = cum[c - 1:c, :]
        diag = jnp.sum(qhs[ci] * kks[ci], axis=-1, keepdims=True)
        oraw_ref[rows, :] = _dot(ps[ci], v) + diag * v
        qc_ref[rows, :] = (qhs[ci] * jnp.exp(cum)).astype(BF16)
        r_ref[rows, :] = _dot_tn(v, kks[ci] * jnp.exp(last - cum))
        el_ref[ci * SUBLANES:(ci + 1) * SUBLANES, :] = jnp.broadcast_to(jnp.exp(last), (SUBLANES, LANES))

    st = st_ref[...]
    for ci in range(n_chunks):
        rows = rows_of(ci)
        oraw_ref[rows, :] += _dot_nt(qc_ref[rows, :], st)
        st = st * el_ref[ci * SUBLANES:ci * SUBLANES + 1, :] + r_ref[rows, :]
    st_ref[...] = st

    o_ref[...] = (_rms(oraw_ref[...], nw) * _silu(g_ref[...])).astype(o_ref.dtype)


def _hgrn(p, lbp, nw, wmat, *, heads, tb):
    b, t, _ = p.shape
    tb = min(tb, t)
    col = lambda off: pl.BlockSpec((None, tb, LANES), lambda bi, h, ti: (bi, ti, off + h))
    return pl.pallas_call(
        _hgrn_kernel,
        out_shape=jax.ShapeDtypeStruct((b, t, heads * HEAD_DIM), BF16),
        grid=(b, heads, t // tb),
        in_specs=[
            col(0), col(heads), col(2 * heads), col(3 * heads),
            pl.BlockSpec((3, LANES), lambda bi, h, ti: (0, h)),
            pl.BlockSpec((1, LANES), lambda bi, h, ti: (0, 0)),
            pl.BlockSpec(wmat.shape, lambda bi, h, ti: (0, 0)),
        ],
        out_specs=pl.BlockSpec((None, tb, LANES), lambda bi, h, ti: (bi, ti, h)),
        scratch_shapes=[
            pltpu.VMEM((HEAD_DIM, HEAD_DIM), F32),
            pltpu.VMEM((tb, HEAD_DIM), F32),
            pltpu.VMEM((tb, HEAD_DIM), BF16),
            pltpu.VMEM((tb, HEAD_DIM), F32),
            pltpu.VMEM((tb // CHUNK * SUBLANES, LANES), F32),
        ],
        compiler_params=_params(("parallel", "parallel", "arbitrary")),
        name="hgrn2",
    )(p, p, p, p, lbp, nw, wmat)


def _gdn_gates_kernel(gm_ref, ap_ref, tri_ref, o_ref, *, heads):
    c = CHUNK
    gm = gm_ref[...]
    xg = gm + ap_ref[1:2, :]
    g_all = -jnp.exp(ap_ref[0:1, :]) * (jnp.maximum(xg, 0.0) + _log1p_exp(-jnp.abs(xg)))
    lane = lax.broadcasted_iota(jnp.int32, (1, LANES), 1)
    beta = jax.nn.sigmoid(gm)
    for ci in range(gm.shape[0] // c):
        rows = slice(ci * c, (ci + 1) * c)
        gcum = jnp.dot(tri_ref[...], _split3(g_all[rows]), preferred_element_type=F32)
        o_ref[rows, :] = jnp.where(lane < heads, beta[rows], gcum)


def _gdn_gates(p, ap, tri, *, heads, gm_col, tb):
    b, t, _ = p.shape
    tb = min(tb, t)
    return pl.pallas_call(
        functools.partial(_gdn_gates_kernel, heads=heads),
        out_shape=jax.ShapeDtypeStruct((b, t, LANES), F32),
        grid=(b, t // tb),
        in_specs=[
            pl.BlockSpec((None, tb, LANES), lambda bi, ti: (bi, ti, gm_col)),
            pl.BlockSpec(ap.shape, lambda bi, ti: (0, 0)),
            pl.BlockSpec(tri.shape, lambda bi, ti: (0, 0)),
        ],
        out_specs=pl.BlockSpec((None, tb, LANES), lambda bi, ti: (bi, ti, 0)),
        compiler_params=_params(("parallel", "parallel")),
        name="gdn_gates",
    )(p, ap, tri)


def _gdn_kernel(q_ref, k_ref, v_ref, g_ref, gm_ref, cq_ref, ck_ref, cv_ref, nw_ref,
                o_ref, s_ref, halo_ref, ext_ref, qn_ref, kn_ref, vn_ref,
                qd_ref, kd_ref, at_ref, gl_ref, r_ref, pq_ref, oraw_ref, *, head_off, hp):
    c = CHUNK
    tb = q_ref.shape[0]
    h0 = pl.program_id(1) * hp
    pad = SUBLANES

    @pl.when(pl.program_id(2) == 0)
    def _():
        s_ref[...] = jnp.zeros_like(s_ref)
        halo_ref[...] = jnp.zeros_like(halo_ref)

    def conv(idx, src_ref, cw_ref):
        ext_ref[0:pad, :] = halo_ref[idx]
        ext_ref[pad:pad + tb, :] = src_ref[...]
        halo_ref[idx] = src_ref[tb - pad:tb, :]
        acc = cw_ref[CONV_WIDTH - 1:CONV_WIDTH, :] * ext_ref[pad:pad + tb, :]
        for j in range(CONV_WIDTH - 1):
            sh = CONV_WIDTH - 1 - j
            acc = acc + cw_ref[j:j + 1, :] * ext_ref[pad - sh:pad - sh + tb, :]
        return _silu(acc)

    def l2n(a):
        return a * lax.rsqrt(jnp.sum(a * a, axis=-1, keepdims=True) + L2_EPS)

    qc, kc = conv(0, q_ref, cq_ref), conv(1, k_ref, ck_ref)
    vn_ref[...] = conv(2, v_ref, cv_ref)
    for hh in range(hp):
        ln = slice(hh * HEAD_DIM, (hh + 1) * HEAD_DIM)
        qn_ref[:, ln] = l2n(qc[:, ln]) * (HEAD_DIM ** -0.5)
        kn_ref[:, ln] = l2n(kc[:, ln])

    nw = nw_ref[...]
    lane = lax.broadcasted_iota(jnp.int32, (1, LANES), 1)
    tt = lax.broadcasted_iota(jnp.int32, (c, c), 0)
    uu = lax.broadcasted_iota(jnp.int32, (c, c), 1)
    hb = _pair_level(tt, uu)

    n_chunks = tb // c
    units = [(hh, ci) for ci in range(n_chunks) for hh in range(hp)]
    rows_of = lambda ci: slice(ci * c, (ci + 1) * c)
    lanes_of = lambda hh: slice(hh * HEAD_DIM, (hh + 1) * HEAD_DIM)
    gl_rows = lambda hh, ci: (hh * n_chunks + ci) * SUBLANES
    nbs, rhss = [], []
    for hh, ci in units:
        rows, ln = rows_of(ci), lanes_of(hh)
        q = qn_ref[rows, ln]
        k = kn_ref[rows, ln]
        v = vn_ref[rows, ln]
        gm = gm_ref[rows, :]

        beta = jnp.sum(gm * (lane == h0 + hh).astype(F32), axis=-1, keepdims=True)
        gcum = jnp.sum(gm * (lane == h0 + hh + head_off).astype(F32), axis=-1, keepdims=True)
        g_last = gcum[c - 1:c, :]

        gb = jnp.broadcast_to(gcum, (c, c))
        decay = jnp.exp(jnp.where(tt >= uu, gb - gb.T, -jnp.inf))
        kb = k * beta
        m = _dot_nt(jnp.concatenate([kb, q], axis=0), k)
        nbs.append((-(m[0:c] * decay)).astype(BF16))

        e_g = jnp.exp(gcum)
        rhss.append(jnp.concatenate([v * beta, kb * e_g], axis=1))
        qd_ref[rows, ln] = q * e_g
        kd_ref[rows, ln] = (k * jnp.exp(g_last - gcum)).astype(BF16)
        at_ref[hh, rows, :] = (m[c:2 * c] * decay).astype(BF16)
        gl_ref[gl_rows(hh, ci):gl_rows(hh, ci) + SUBLANES, :] = jnp.broadcast_to(jnp.exp(g_last), (SUBLANES, LANES))

    mis = [jnp.where(hb == 0, nb, 0).astype(F32) for nb in nbs]
    one_b = jnp.ones((c, c), BF16)
    for lv in range(1, N_LEVELS):
        o_lvs = [jnp.where(hb == lv, nb, 0) for nb in nbs]
        mbs = [mi.astype(BF16) for mi in mis]
        ys = [jnp.dot(jnp.where(tt == uu, one_b, mb), o_lv, preferred_element_type=F32)
              for mb, o_lv in zip(mbs, o_lvs)]
        mis = [mi + y + jnp.dot(y.astype(BF16), mb, preferred_element_type=F32)
               for mi, y, mb in zip(mis, ys, mbs)]

    sols = [rhs + _dot(mi, rhs) for mi, rhs in zip(mis, rhss)]
    for (hh, ci), sol in zip(units, sols):
        rows, ln = rows_of(ci), lanes_of(hh)
        sol = sol.astype(BF16)
        kd_uw = _dot_tn(kd_ref[rows, ln], sol)
        at_uw = jnp.dot(at_ref[hh, rows, :], sol, preferred_element_type=F32)
        r_ref[hh, rows, :] = kd_uw[:, 0:HEAD_DIM]
        pq_ref[hh, ci, 0:c, :] = kd_uw[:, HEAD_DIM:2 * HEAD_DIM].astype(BF16)
        pq_ref[hh, ci, c:2 * c, :] = (qd_ref[rows, ln] - at_uw[:, HEAD_DIM:2 * HEAD_DIM]).astype(BF16)
        oraw_ref[rows, ln] = at_uw[:, 0:HEAD_DIM]

    ss = [s_ref[hh] for hh in range(hp)]
    for ci in range(n_chunks):
        rows = rows_of(ci)
        zs = [jnp.dot(pq_ref[hh, ci], ss[hh].astype(BF16), preferred_element_type=F32) for hh in range(hp)]
        for hh in range(hp):
            oraw_ref[rows, lanes_of(hh)] += zs[hh][c:2 * c]
            ss[hh] = (ss[hh] * gl_ref[gl_rows(hh, ci):gl_rows(hh, ci) + 1, :] - zs[hh][0:c]
                      + r_ref[hh, rows, :])
    for hh in range(hp):
        s_ref[hh] = ss[hh]

    g = _silu(g_ref[...])
    for hh in range(hp):
        ln = lanes_of(hh)
        o_ref[:, ln] = (_rms(oraw_ref[:, ln], nw) * g[:, ln]).astype(o_ref.dtype)


def _gdn(p, gates, conv_w, nw, *, heads, col0, tb, hp):
    b, t, _ = p.shape
    tb = min(tb, t)
    w = hp * HEAD_DIM
    n_chunks = tb // CHUNK
    col = lambda off: pl.BlockSpec((None, tb, w), lambda bi, h, ti: (bi, ti, off // hp + h))
    cw = lambda off: pl.BlockSpec((CONV_WIDTH, w), lambda bi, h, ti: (0, off // hp + h))
    return pl.pallas_call(
        functools.partial(_gdn_kernel, head_off=heads, hp=hp),
        out_shape=jax.ShapeDtypeStruct((b, t, heads * HEAD_DIM), BF16),
        grid=(b, heads // hp, t // tb),
        in_specs=[
            col(col0), col(col0 + heads), col(col0 + 2 * heads), col(col0 + 3 * heads),
            pl.BlockSpec((None, tb, LANES), lambda bi, h, ti: (bi, ti, 0)),
            cw(0), cw(heads), cw(2 * heads),
            pl.BlockSpec(nw.shape, lambda bi, h, ti: (0, 0)),
        ],
        out_specs=pl.BlockSpec((None, tb, w), lambda bi, h, ti: (bi, ti, h)),
        scratch_shapes=[
            pltpu.VMEM((hp, HEAD_DIM, HEAD_DIM), F32),
            pltpu.VMEM((3, SUBLANES, w), F32),
            pltpu.VMEM((tb + SUBLANES, w), F32),
            pltpu.VMEM((tb, w), F32),
            pltpu.VMEM((tb, w), F32),
            pltpu.VMEM((tb, w), F32),
            pltpu.VMEM((tb, w), F32),
            pltpu.VMEM((tb, w), BF16),
            pltpu.VMEM((hp, tb, CHUNK), BF16),
            pltpu.VMEM((hp * n_chunks * SUBLANES, LANES), F32),
            pltpu.VMEM((hp, tb, HEAD_DIM), F32),
            pltpu.VMEM((hp, n_chunks, 2 * CHUNK, HEAD_DIM), BF16),
            pltpu.VMEM((tb, w), F32),
        ],
        compiler_params=_params(("parallel", "parallel", "arbitrary")),
        name="gated_deltanet",
    )(p, p, p, p, gates, conv_w, conv_w, conv_w, nw)


def _pool_kernel(u_ref, w_ref, sc_ref, o_ref, ext_ref):
    tb = u_ref.shape[0]
    ti = pl.program_id(1)

    @pl.when(ti == 0)
    def _():
        ext_ref[0:POOL_MAX, :] = jnp.zeros((POOL_MAX, ext_ref.shape[1]), F32)

    ext_ref[POOL_MAX:POOL_MAX + tb, :] = u_ref[...]
    pos = ti * tb + lax.broadcasted_iota(jnp.int32, (tb, 1), 0)
    for gi, win in enumerate(POOL_WINDOWS):
        sl = slice(gi * LANES, (gi + 1) * LANES)
        cur = ext_ref[POOL_MAX:POOL_MAX + tb, sl]
        ws = cur
        for j in range(1, win):
            ws = ws + ext_ref[POOL_MAX - j:POOL_MAX - j + tb, sl]
        count = jnp.minimum(pos + 1, win).astype(F32)
        m = ws / count - cur
        o_ref[:, sl] = (_dot(m, w_ref[gi]) * sc_ref[:, sl]).astype(o_ref.dtype)
    ext_ref[0:POOL_MAX, :] = ext_ref[tb:tb + POOL_MAX, :]


def _pool_gates_kernel(u_ref, w_ref, sc_ref, gm_ref, ap_ref, tri_ref, o_ref, og_ref, ext_ref, *, heads):
    _pool_kernel(u_ref, w_ref, sc_ref, o_ref, ext_ref)
    _gdn_gates_kernel(gm_ref, ap_ref, tri_ref, og_ref, heads=heads)


def _pool_gates(p, pool_w, scale, ap, tri, *, colblk, gm_col, heads, tb):
    b, t, _ = p.shape
    tb = min(tb, t)
    width = len(POOL_WINDOWS) * LANES
    return pl.pallas_call(
        functools.partial(_pool_gates_kernel, heads=heads),
        out_shape=(jax.ShapeDtypeStruct((b, t, width), BF16), jax.ShapeDtypeStruct((b, t, LANES), F32)),
        grid=(b, t // tb),
        in_specs=[
            pl.BlockSpec((None, tb, width), lambda bi, ti: (bi, ti, colblk)),
            pl.BlockSpec(pool_w.shape, lambda bi, ti: (0, 0, 0)),
            pl.BlockSpec((1, width), lambda bi, ti: (0, 0)),
            pl.BlockSpec((None, tb, LANES), lambda bi, ti: (bi, ti, gm_col)),
            pl.BlockSpec(ap.shape, lambda bi, ti: (0, 0)),
            pl.BlockSpec(tri.shape, lambda bi, ti: (0, 0)),
        ],
        out_specs=(pl.BlockSpec((None, tb, width), lambda bi, ti: (bi, ti, 0)),
                   pl.BlockSpec((None, tb, LANES), lambda bi, ti: (bi, ti, 0))),
        scratch_shapes=[pltpu.VMEM((tb + POOL_MAX, width), F32)],
        compiler_params=_params(("parallel", "arbitrary")),
        name="causal_pool_and_gates",
    )(p, pool_w, scale, p, ap, tri)


def kernel(x, lb_logits, norm_ffn1, ffn1_w_gate, ffn1_w_up, ffn1_w_down, norm_mix, w_in, gdn_conv_w, gdn_a_log, gdn_dt_bias, hgrn_norm_w, gdn_norm_w, pool_w, pool_scale, w_out, norm_ffn2, ffn2_w_gate, ffn2_w_up, ffn2_w_down, norm_final):
    bsz, seq, d = x.shape
    depth = lb_logits.shape[0]
    hgrn_w = lb_logits.shape[1]
    gdn_heads = gdn_a_log.shape[1]
    gdn_w = gdn_heads * HEAD_DIM
    pool_width = pool_scale.shape[1]
    hgrn_heads = hgrn_w // HEAD_DIM
    n = bsz * seq

    main = 4 * hgrn_w + 4 * gdn_w
    e_used = main + pool_width + LANES
    tn_in = TILES["inproj"]["tn"]
    e_pad = -(-e_used // tn_in) * tn_in
    w_in_r = _w_in_prep(w_in, main=main, nscal=2 * gdn_heads, e_pad=e_pad, **TILES["w_in_prep"])
    gdn_col0 = 4 * hgrn_heads
    pool_colblk = main // pool_width
    gm_col = (main + pool_width) // LANES

    lane_pad = lambda a: jnp.pad(a, ((0, 0), (gdn_heads, LANES - 2 * gdn_heads)))
    gdn_ap = jnp.stack([lane_pad(gdn_a_log), lane_pad(gdn_dt_bias)], axis=1)

    hgrn_wmat = jnp.asarray(_hgrn_decay_matrix(), BF16)
    tri = np.tril(np.ones((CHUNK, CHUNK), np.float32))
    tri3 = jnp.asarray(np.concatenate([tri, tri, tri], axis=1), BF16)

    lbp = _lb_params(lb_logits.astype(F32))

    w_out_b = w_out.astype(BF16)

    ffn1_w = [w.astype(BF16) for w in (ffn1_w_gate, ffn1_w_up, ffn1_w_down)]
    ffn2_w = [w.astype(BF16) for w in (ffn2_w_gate, ffn2_w_up, ffn2_w_down)]
    pool_w_b = pool_w.astype(BF16)

    xf = x.reshape(n, d)
    row = lambda a: a.reshape(1, -1)
    for l in range(depth):
        xf = _ffn(xf, row(norm_ffn1[l]), *ffn1_w, row(norm_final), layer=l, final_norm=False, **TILES["ffn"])
        p = _inproj(xf, row(norm_mix[l]), w_in_r, layer=l, **TILES["inproj"]).reshape(bsz, seq, e_pad)
        ya = _hgrn(p, lbp[l], row(hgrn_norm_w[l]), hgrn_wmat, heads=hgrn_heads, **TILES["hgrn"])
        yc, gates = _pool_gates(p, pool_w_b[l], row(pool_scale[l]), gdn_ap[l], tri3, colblk=pool_colblk,
                                gm_col=gm_col, heads=gdn_heads, **TILES["pool"])
        yb = _gdn(p, gates, gdn_conv_w[l], row(gdn_norm_w[l]), heads=gdn_heads, col0=gdn_col0, **TILES["gdn"])
        xf = _outproj(xf, ya.reshape(n, -1), yb.reshape(n, -1), yc.reshape(n, -1), w_out_b, layer=l,
                      **TILES["outproj"])
        xf = _ffn(xf, row(norm_ffn2[l]), *ffn2_w, row(norm_final), layer=l,
                  final_norm=(l == depth - 1), **TILES["ffn"])
    return xf.reshape(bsz, seq, d)
```
